```python
import jax, jax.numpy as jnp
from jax import lax
import numpy as np

D_MODEL = 2048
BATCH = 4
SEQ = 2048
DEPTH = 2

GRID_W = 64
CTX_LEN = 256

D_A = 1024
RWKV_HEAD = 64
H_A = D_A // RWKV_HEAD
W_LORA = 64
A_LORA = 64
G_LORA = 128
LNX_EPS = 64e-5
D_B = 1024
HEAD_DIM = 128
H_Q = D_B // HEAD_DIM
H_KV = 2
Q_PER_KV = H_Q // H_KV
ROPE_AXIS_DIM = HEAD_DIM // 2
ROPE_THETA = 10000.0
Q_BLOCK = 128
RWKV_SPLITS = [D_A, 2 * D_A, 3 * D_A, 3 * D_A + W_LORA, 3 * D_A + W_LORA + A_LORA]
A_COLS = 3 * D_A + W_LORA + A_LORA + G_LORA
B_COLS = D_B + 2 * H_KV * HEAD_DIM
IN_COLS = A_COLS + B_COLS
D_MIX = D_A + D_B
FOURIER_GROUPS = 8
FOURIER_GROUP_DIM = D_MODEL // FOURIER_GROUPS
N_EXPERTS = 16
N_EXPERT_GROUPS = 4
EXPERTS_PER_GROUP = N_EXPERTS // N_EXPERT_GROUPS
TOP_K = 2
D_EXPERT = 512
N_EVEN = (DEPTH + 1) // 2
N_ODD = DEPTH // 2
NORM_EPS = 1e-6

kernel_name = 'hybrid_rwkv7_gqa_fnet_moe_dit'


def rms_norm(x, gain, eps=NORM_EPS):
    xf = x.astype(jnp.float32)
    y = xf * lax.rsqrt(jnp.mean(xf * xf, axis=-1, keepdims=True) + eps)
    return (y * gain.astype(jnp.float32)).astype(x.dtype)


def centred_shift(p, mu_prev, mu_next):
    zero = jnp.zeros_like(p[:, :1])
    prev = jnp.concatenate([zero, p[:, :-1]], axis=1)
    nxt = jnp.concatenate([p[:, 1:], zero], axis=1)
    return p + mu_prev * (prev - p) + mu_next * (nxt - p)


def axial_rope(n):
    rows = n // GRID_W
    row = jnp.repeat(jnp.arange(rows), GRID_W).astype(jnp.float32)
    col = jnp.tile(jnp.arange(GRID_W), rows).astype(jnp.float32)
    inv = ROPE_THETA ** (-jnp.arange(0, ROPE_AXIS_DIM, 2, dtype=jnp.float32) / ROPE_AXIS_DIM)
    ang = jnp.concatenate([row[:, None] * inv, col[:, None] * inv], axis=-1)
    return jnp.cos(ang), jnp.sin(ang)


def apply_rope(x, cos, sin):
    xf = x.astype(jnp.float32).reshape(x.shape[:-1] + (HEAD_DIM // 2, 2))
    x0, x1 = xf[..., 0], xf[..., 1]
    bshape = cos.shape[:1] + (1,) * (x.ndim - 3) + cos.shape[1:]
    c, s = cos.reshape(bshape), sin.reshape(bshape)
    out = jnp.stack([x0 * c - x1 * s, x0 * s + x1 * c], axis=-1)
    return out.reshape(x.shape).astype(x.dtype)


def wkv7_scan(s0, r, w, k, v, a, b, reverse):
    def step(s, inp):
        r_t, w_t, k_t, v_t, a_t, b_t = inp
        sa = jnp.einsum('bhij,bhj->bhi', s, a_t)
        s = s * w_t[:, :, None, :] + sa[..., None] * b_t[:, :, None, :] + v_t[..., None] * k_t[:, :, None, :]
        return s, jnp.einsum('bhij,bhj->bhi', s, r_t)
    xs = tuple(jnp.moveaxis(t, 1, 0) for t in (r, w, k, v, a, b))
    s_fin, ys = lax.scan(step, s0, xs, reverse=reverse)
    return s_fin, jnp.moveaxis(ys, 0, 1)


def rwkv7_streams(pa, w0, w2, a0, a2, g2, k_k, k_a):
    bsz, t = pa.shape[:2]
    heads = lambda z: z.astype(jnp.float32).reshape(bsz, t, H_A, RWKV_HEAD)
    r, k, v, xw, xa, xg = jnp.split(pa, RWKV_SPLITS, axis=-1)
    g = jax.nn.sigmoid(xg) @ g2
    kk = heads(k * k_k)
    kk = kk * lax.rsqrt(jnp.sum(kk * kk, axis=-1, keepdims=True) + 1e-12)
    dirs = []
    for d in range(2):
        w_log = -jax.nn.softplus(-(w0[d] + jnp.tanh(xw) @ w2[d])) - 0.5
        decay = jnp.exp(-jnp.exp(w_log.astype(jnp.float32)))
        a = jax.nn.sigmoid(a0[d] + xa @ a2[d])
        k_d = k * (1 + (a - 1) * k_a)
        dirs.append((heads(decay), heads(k_d), heads(a)))
    return heads(r), heads(v), kk, g, dirs


def rwkv7_finish(y, r, v, k_dirs, g, ln_g, ln_b, r_k):
    bsz, t = y.shape[:2]
    mu = jnp.mean(y, axis=-1, keepdims=True)
    var = jnp.mean(jnp.square(y - mu), axis=-1, keepdims=True)
    yn = ((y - mu) * lax.rsqrt(var + LNX_EPS)).reshape(bsz, t, D_A) * ln_g + ln_b
    kb = k_dirs[0] + k_dirs[1]
    bonus = (jnp.sum(r * kb * r_k, axis=-1, keepdims=True) * v).reshape(bsz, t, D_A)
    return ((yn + bonus) * g).astype(g.dtype)


def even_mixer(h_ctx, h_lat, w_in, mu_prev, mu_next, w0, w2, a0, a2, g2, k_k, k_a, r_k, ln_g, ln_b,
               q_gain, k_gain, w_out, cos, sin, ctx_out):
    p_ctx = h_ctx @ w_in
    p_lat = h_lat @ w_in
    pa_c = centred_shift(p_ctx[..., :A_COLS], mu_prev, mu_next)
    pa_l = centred_shift(p_lat[..., :A_COLS], mu_prev, mu_next)
    pb_c, pb_l = p_ctx[..., A_COLS:], p_lat[..., A_COLS:]

    rc, vc, kkc, gc, dc = rwkv7_streams(pa_c, w0, w2, a0, a2, g2, k_k, k_a)
    rl, vl, kkl, gl, dl = rwkv7_streams(pa_l, w0, w2, a0, a2, g2, k_k, k_a)
    s0 = jnp.zeros((h_lat.shape[0], H_A, RWKV_HEAD, RWKV_HEAD), jnp.float32)
    ys_c, ys_l = [], []
    for d, rev in enumerate((False, True)):
        dec_c, k_c, a_c = dc[d]
        dec_l, k_l, a_l = dl[d]
        s_c, y_c = wkv7_scan(s0, rc, dec_c, k_c, vc, -kkc, kkc * a_c, rev)
        _, y_l = wkv7_scan(s_c, rl, dec_l, k_l, vl, -kkl, kkl * a_l, rev)
        ys_c.append(y_c)
        ys_l.append(y_l)
    ya_l = rwkv7_finish(ys_l[0] + ys_l[1], rl, vl, [dl[0][1], dl[1][1]], gl, ln_g, ln_b, r_k)

    def qkv(pb):
        bsz, t = pb.shape[:2]
        q, k, v = jnp.split(pb, [D_B, D_B + H_KV * HEAD_DIM], axis=-1)
        q = rms_norm(q.reshape(bsz, t, H_KV, Q_PER_KV, HEAD_DIM), q_gain)
        k = rms_norm(k.reshape(bsz, t, H_KV, HEAD_DIM), k_gain)
        return q, k, v.reshape(bsz, t, H_KV, HEAD_DIM)
    qc, kc, vc_b = qkv(pb_c)
    ql, kl, vl_b = qkv(pb_l)
    ql = apply_rope(ql, cos, sin)
    kl = apply_rope(kl, cos, sin)
    k_all = jnp.concatenate([kc, kl], axis=1)
    v_all = jnp.concatenate([vc_b, vl_b], axis=1)
    scale = HEAD_DIM ** -0.5

    def attend(qb, keys, vals):
        s = jnp.einsum('bqkgd,bskd->bkgqs', qb, keys, preferred_element_type=jnp.float32) * scale
        p = jax.nn.softmax(s, axis=-1)
        return jnp.einsum('bkgqs,bskd->bqkgd', p.astype(vals.dtype), vals)

    bsz, seq = ql.shape[:2]
    nb = seq // Q_BLOCK
    qblocks = jnp.moveaxis(ql.reshape(bsz, nb, Q_BLOCK, H_KV, Q_PER_KV, HEAD_DIM), 1, 0)
    o_l = lax.map(lambda qb: attend(qb, k_all, v_all), qblocks)
    yb_l = jnp.moveaxis(o_l, 0, 1).reshape(bsz, seq, D_B)

    out_lat = jnp.concatenate([ya_l, yb_l.astype(ya_l.dtype)], axis=-1) @ w_out
    out_ctx = None
    if ctx_out:
        ya_c = rwkv7_finish(ys_c[0] + ys_c[1], rc, vc, [dc[0][1], dc[1][1]], gc, ln_g, ln_b, r_k)
        yb_c = attend(qc, kc, vc_b).reshape(bsz, qc.shape[1], D_B)
        out_ctx = jnp.concatenate([ya_c, yb_c.astype(ya_c.dtype)], axis=-1) @ w_out
    return out_lat, out_ctx


def fourier_mixer(h, w_out):
    bsz, t, _ = h.shape
    hg = h.astype(jnp.float32).reshape(bsz, t, FOURIER_GROUPS, FOURIER_GROUP_DIM)
    f = jnp.fft.fftn(hg, axes=(1, 3), norm='ortho').real
    return f.reshape(bsz, t, D_MODEL).astype(h.dtype) @ w_out


def route_grouped_top2(h, router_w, router_b):
    s = jax.nn.sigmoid((h @ router_w).astype(jnp.float32))
    sel = s + router_b.astype(jnp.float32)
    grp = sel.reshape(-1, N_EXPERT_GROUPS, EXPERTS_PER_GROUP)
    grp_score = jnp.sum(lax.top_k(grp, TOP_K)[0], axis=-1)
    best = jnp.argmax(grp_score, axis=-1)
    in_grp = jnp.repeat(best[:, None] == jnp.arange(N_EXPERT_GROUPS), EXPERTS_PER_GROUP, axis=-1)
    _, idx = lax.top_k(jnp.where(in_grp, sel, -jnp.inf), TOP_K)
    w = jnp.take_along_axis(s, idx, axis=-1)
    w = w / jnp.sum(w, axis=-1, keepdims=True)
    return jnp.sum(jax.nn.one_hot(idx, N_EXPERTS, dtype=jnp.float32) * w[..., None], axis=1)


def moe_ffn(h, router_w, router_b, w_gate, w_up, w_down):
    gates = route_grouped_top2(h, router_w, router_b).astype(h.dtype)
    out = jnp.zeros_like(h)
    for e in range(N_EXPERTS):
        u = jax.nn.silu(h @ w_gate[e]) * (h @ w_up[e])
        out = out + gates[:, e:e + 1] * (u @ w_down[e])
    return out


def setup_inputs(seed: int = 0) -> dict:
    key = jax.random.key(seed)
    ks = iter(list(jax.random.split(key, 40)))
    f32 = jnp.float32
    nrm = lambda shape, s: jax.random.normal(next(ks), shape, f32) * s
    uni = lambda shape: jax.random.uniform(next(ks), shape, f32, 0.0, 0.5)
    return {
        'x': nrm((BATCH, SEQ, D_MODEL), 1.0),
        'c': nrm((BATCH, D_MODEL), 1.0),
        'ctx': nrm((BATCH, CTX_LEN, D_MODEL), 1.0),
        'c_ctx': nrm((D_MODEL,), 1.0),
        'mod_w': nrm((DEPTH, D_MODEL, 6 * D_MODEL), 0.5 * D_MODEL ** -0.5),
        'mod_b': nrm((DEPTH, 6 * D_MODEL), 0.01),
        'norm_mix': 1.0 + nrm((DEPTH, D_MODEL), 0.02),
        'norm_ffn': 1.0 + nrm((DEPTH, D_MODEL), 0.02),
        'ab_w_in': nrm((N_EVEN, D_MODEL, IN_COLS), D_MODEL ** -0.5),
        'ab_mu_prev': uni((N_EVEN, A_COLS)),
        'ab_mu_next': uni((N_EVEN, A_COLS)),
        'rwkv_w0': nrm((N_EVEN, 2, D_A), 0.5),
        'rwkv_w2': nrm((N_EVEN, 2, W_LORA, D_A), 0.1),
        'rwkv_a0': nrm((N_EVEN, 2, D_A), 0.5),
        'rwkv_a2': nrm((N_EVEN, 2, A_LORA, D_A), A_LORA ** -0.5),
        'rwkv_g2': nrm((N_EVEN, G_LORA, D_A), G_LORA ** -0.5),
        'rwkv_k_k': 0.85 + nrm((N_EVEN, D_A), 0.02),
        'rwkv_k_a': 1.0 + nrm((N_EVEN, D_A), 0.02),
        'rwkv_r_k': nrm((N_EVEN, H_A, RWKV_HEAD), 0.1),
        'rwkv_ln_g': 1.0 + nrm((N_EVEN, D_A), 0.02),
        'rwkv_ln_b': nrm((N_EVEN, D_A), 0.01),
        'attn_q_norm': 1.0 + nrm((N_EVEN, HEAD_DIM), 0.02),
        'attn_k_norm': 1.0 + nrm((N_EVEN, HEAD_DIM), 0.02),
        'ab_w_out': nrm((N_EVEN, D_MIX, D_MODEL), D_MIX ** -0.5),
        'fourier_w_out': nrm((N_ODD, D_MODEL, D_MODEL), D_MODEL ** -0.5),
        'router_w': nrm((D_MODEL, N_EXPERTS), D_MODEL ** -0.5),
        'router_b': nrm((N_EXPERTS,), 0.01),
        'exp_w_gate': nrm((DEPTH, N_EXPERTS, D_MODEL, D_EXPERT), D_MODEL ** -0.5),
        'exp_w_up': nrm((DEPTH, N_EXPERTS, D_MODEL, D_EXPERT), D_MODEL ** -0.5),
        'exp_w_down': nrm((DEPTH, N_EXPERTS, D_EXPERT, D_MODEL), D_EXPERT ** -0.5),
    }


def reference(x, c, ctx, c_ctx, mod_w, mod_b, norm_mix, norm_ffn, ab_w_in, ab_mu_prev, ab_mu_next,
              rwkv_w0, rwkv_w2, rwkv_a0, rwkv_a2, rwkv_g2, rwkv_k_k, rwkv_k_a, rwkv_r_k, rwkv_ln_g, rwkv_ln_b,
              attn_q_norm, attn_k_norm, ab_w_out, fourier_w_out, router_w, router_b,
              exp_w_gate, exp_w_up, exp_w_down):
    seq = x.shape[1]
    cos, sin = axial_rope(seq)
    s_lat = jax.nn.silu(c)
    s_ctx = jax.nn.silu(c_ctx)
    for l in range(DEPTH):
        ctx_out = any(j % 2 == 0 for j in range(l + 1, DEPTH))
        uses_ctx = (l % 2 == 0) or ctx_out
        m_lat = (s_lat @ mod_w[l] + mod_b[l])[:, None, :]
        sh1, sc1, gt1, sh2, sc2, gt2 = jnp.split(m_lat, 6, axis=-1)
        h_lat = rms_norm(x, norm_mix[l]) * (1 + sc1) + sh1
        if uses_ctx:
            m_ctx = s_ctx @ mod_w[l] + mod_b[l]
            csh1, csc1, cgt1, csh2, csc2, cgt2 = jnp.split(m_ctx, 6)
            h_ctx = rms_norm(ctx, norm_mix[l]) * (1 + csc1) + csh1
        if l % 2 == 0:
            e = l // 2
            y_lat, y_ctx = even_mixer(h_ctx, h_lat, ab_w_in[e], ab_mu_prev[e], ab_mu_next[e],
                                      rwkv_w0[e], rwkv_w2[e], rwkv_a0[e], rwkv_a2[e], rwkv_g2[e],
                                      rwkv_k_k[e], rwkv_k_a[e], rwkv_r_k[e], rwkv_ln_g[e], rwkv_ln_b[e],
                                      attn_q_norm[e], attn_k_norm[e], ab_w_out[e], cos, sin, ctx_out)
        else:
            o = l // 2
            y_lat = fourier_mixer(h_lat, fourier_w_out[o])
            y_ctx = fourier_mixer(h_ctx, fourier_w_out[o]) if ctx_out else None
        x = x + gt1 * y_lat
        tokens = (rms_norm(x, norm_ffn[l]) * (1 + sc2) + sh2).reshape(-1, D_MODEL)
        if ctx_out:
            ctx = ctx + cgt1 * y_ctx
            h2c = rms_norm(ctx, norm_ffn[l]) * (1 + csc2) + csh2
            tokens = jnp.concatenate([tokens, h2c.reshape(-1, D_MODEL)], axis=0)
        f = moe_ffn(tokens, router_w, router_b, exp_w_gate[l], exp_w_up[l], exp_w_down[l])
        n_lat = x.shape[0] * seq
        x = x + gt2 * f[:n_lat].reshape(x.shape)
        if ctx_out:
            ctx = ctx + cgt2 * f[n_lat:].reshape(ctx.shape)
    return x
```

```python
import functools

import jax
import jax.numpy as jnp
import numpy as np
from jax import lax
from jax.experimental import pallas as pl
from jax.experimental.pallas import tpu as pltpu

F32 = jnp.float32
BF16 = jnp.bfloat16

D_MODEL = 2048
D_A = 1024
RWKV_HEAD = 64
W_LORA = 64
A_LORA = 64
G_LORA = 128
LNX_EPS = 64e-5
D_B = 1024
HEAD_DIM = 128
H_Q = 8
H_KV = 2
Q_PER_KV = H_Q // H_KV
GRID_W = 64
ROPE_THETA = 10000.0
ROPE_AXIS_DIM = HEAD_DIM // 2
FOURIER_GROUPS = 8
FOURIER_GROUP_DIM = D_MODEL // FOURIER_GROUPS
N_EXPERTS = 16
N_EXPERT_GROUPS = 4
EXPERTS_PER_GROUP = 4
D_EXPERT = 512
NORM_EPS = 1e-6
KK_EPS = 1e-12

LANES = 128
WKV_CHUNK = 64
WKV_SUB = 4
VMEM_LIMIT = 56 * 1024 * 1024

NT_DIMS = (((1,), (1,)), ((), ()))
TN_DIMS = (((0,), (0,)), ((), ()))


def _params(sem):
    return pltpu.CompilerParams(dimension_semantics=sem, vmem_limit_bytes=VMEM_LIMIT)


def _bdot(a, b, dims=None):
    a = a.astype(BF16)
    b = b.astype(BF16)
    if dims is None:
        return jnp.dot(a, b, preferred_element_type=F32)
    return lax.dot_general(a, b, dims, preferred_element_type=F32)


def _split2(x):
    hi = x.astype(BF16)
    lo = (x - hi.astype(F32)).astype(BF16)
    return hi, lo


def _split3(x):
    hi = x.astype(BF16)
    r1 = x - hi.astype(F32)
    mid = r1.astype(BF16)
    lo = (r1 - mid.astype(F32)).astype(BF16)
    return hi, mid, lo


def _dot3(a, b, dims=None):
    ah, al = _split2(a)
    bh, bl = _split2(b)
    return _bdot(ah, bh, dims) + (_bdot(ah, bl, dims) + _bdot(al, bh, dims))


def _dot_exact_lhs(a_bf16, x, dims=None):
    h, m, l = _split3(x)
    return _bdot(a_bf16, h, dims) + (_bdot(a_bf16, m, dims) + _bdot(a_bf16, l, dims))


def _dot_exact_rhs(x, b_bf16):
    h, m, l = _split3(x)
    return _bdot(h, b_bf16) + (_bdot(m, b_bf16) + _bdot(l, b_bf16))


def _sigmoid(x):
    return 1.0 / (1.0 + jnp.exp(-x))


def _silu(x):
    return x * _sigmoid(x)


def _mod_kernel(c_ref, w_ref, b_ref, o_ref):
    s = _silu(c_ref[...])
    o_ref[0] = _bdot(s, w_ref[0]) + b_ref[0]


def _modulation(cvec, mod_w, mod_b):
    depth, d, n = mod_w.shape
    tn = 1024
    return pl.pallas_call(
        _mod_kernel,
        grid=(depth, n // tn),
        in_specs=[pl.BlockSpec((8, d), lambda l, j: (0, 0)),
                  pl.BlockSpec((1, d, tn), lambda l, j: (l, 0, j)),
                  pl.BlockSpec((1, 1, tn), lambda l, j: (l, 0, j))],
        out_specs=pl.BlockSpec((1, 8, tn), lambda l, j: (l, 0, j)),
        out_shape=jax.ShapeDtypeStruct((depth, 8, n), F32),
        compiler_params=_params(("parallel", "parallel")),
        name="modulation",
    )(cvec, mod_w, mod_b.reshape(depth, 1, n))


def _normed(x, gain, sc, sh):
    y = x * lax.rsqrt(jnp.mean(x * x, axis=-1, keepdims=True) + NORM_EPS)
    return (y * gain) * (1.0 + sc) + sh


def _norm_mod_kernel(x_ref, g_ref, sc_ref, sh_ref, o_ref):
    o_ref[0] = _normed(x_ref[0], g_ref[...], sc_ref[0], sh_ref[0]).astype(BF16)


def _top2_gates(logits, bias):
    s = _sigmoid(logits)
    sel = s + bias
    neg = jnp.float32(-jnp.inf)
    flags, scores = [], []
    for g in range(N_EXPERT_GROUPS):
        xs = [sel[g * EXPERTS_PER_GROUP + i:g * EXPERTS_PER_GROUP + i + 1] for i in range(EXPERTS_PER_GROUP)]
        m1 = functools.reduce(jnp.maximum, xs)
        first, taken = [], None
        for x in xs:
            f = (x == m1) if taken is None else jnp.logical_and(x == m1, jnp.logical_not(taken))
            taken = f if taken is None else jnp.logical_or(taken, f)
            first.append(f)
        rest = [jnp.where(f, neg, x) for f, x in zip(first, xs)]
        m2 = functools.reduce(jnp.maximum, rest)
        second, taken = [], None
        for f, x in zip(first, xs):
            c = jnp.logical_and(x == m2, jnp.logical_not(f))
            if taken is not None:
                c = jnp.logical_and(c, jnp.logical_not(taken))
            taken = c if taken is None else jnp.logical_or(taken, c)
            second.append(c)
        flags.append([jnp.logical_or(f, c) for f, c in zip(first, second)])
        scores.append(m1 + m2)
    best = functools.reduce(jnp.maximum, scores)
    taken, rows = None, []
    for g in range(N_EXPERT_GROUPS):
        bg = (scores[g] == best) if taken is None else jnp.logical_and(scores[g] == best, jnp.logical_not(taken))
        taken = bg if taken is None else jnp.logical_or(taken, bg)
        for i in range(EXPERTS_PER_GROUP):
            e = g * EXPERTS_PER_GROUP + i
            rows.append(jnp.where(jnp.logical_and(bg, flags[g][i]), s[e:e + 1], 0.0))
    denom = functools.reduce(jnp.add, rows)
    inv = 1.0 / denom
    return [r * inv for r in rows]


def _norm_mod_router_kernel(x_ref, g_ref, sc_ref, sh_ref, rw_ref, rb_ref, o_ref, gates_ref):
    h = _normed(x_ref[0], g_ref[...], sc_ref[0], sh_ref[0])
    o_ref[0] = h.astype(BF16)
    logits = _dot3(rw_ref[...], h, NT_DIMS)
    for e, row in enumerate(_top2_gates(logits, rb_ref[...])):
        gates_ref[0, e:e + 1, :] = row


def _norm_mod(x, gain, sc, sh, tm=512):
    b, t, d = x.shape
    tm = min(tm, t)
    per_b = sc.shape[0] > 1
    mod_map = (lambda bi, i: (bi, 0, 0)) if per_b else (lambda bi, i: (0, 0, 0))
    return pl.pallas_call(
        _norm_mod_kernel,
        grid=(b, t // tm),
        in_specs=[pl.BlockSpec((1, tm, d), lambda bi, i: (bi, i, 0)),
                  pl.BlockSpec((1, d), lambda bi, i: (0, 0)),
                  pl.BlockSpec((1, 1, d), mod_map),
                  pl.BlockSpec((1, 1, d), mod_map)],
        out_specs=pl.BlockSpec((1, tm, d), lambda bi, i: (bi, i, 0)),
        out_shape=jax.ShapeDtypeStruct((b, t, d), BF16),
        compiler_params=_params(("parallel", "parallel")),
        name="norm_mod",
    )(x, gain.reshape(1, d), sc, sh)


def _norm_mod_router(x, gain, sc, sh, router_w_t, router_b, tm=512):
    b, t, d = x.shape
    tm = min(tm, t)
    e = router_w_t.shape[0]
    return pl.pallas_call(
        _norm_mod_router_kernel,
        grid=(b, t // tm),
        in_specs=[pl.BlockSpec((1, tm, d), lambda bi, i: (bi, i, 0)),
                  pl.BlockSpec((1, d), lambda bi, i: (0, 0)),
                  pl.BlockSpec((1, 1, d), lambda bi, i: (bi, 0, 0)),
                  pl.BlockSpec((1, 1, d), lambda bi, i: (bi, 0, 0)),
                  pl.BlockSpec((e, d), lambda bi, i: (0, 0)),
                  pl.BlockSpec((e, 1), lambda bi, i: (0, 0))],
        out_specs=[pl.BlockSpec((1, tm, d), lambda bi, i: (bi, i, 0)),
                   pl.BlockSpec((1, e, tm), lambda bi, i: (bi, 0, i))],
        out_shape=[jax.ShapeDtypeStruct((b, t, d), BF16),
                   jax.ShapeDtypeStruct((b, e, t), F32)],
        compiler_params=_params(("parallel", "parallel")),
        name="norm_mod_router",
    )(x, gain.reshape(1, d), sc, sh, router_w_t, router_b.reshape(e, 1))


def _mm_kernel(*refs, n_terms, a_batched, w_batched, has_res):
    a_refs = refs[:n_terms]
    w_refs = refs[n_terms:2 * n_terms]
    o_ref = refs[-1]
    acc = None
    for a_ref, w_ref, ab, wb in zip(a_refs, w_refs, a_batched, w_batched):
        a = a_ref[0] if ab else a_ref[...]
        w = w_ref[0] if wb else w_ref[...]
        p = jnp.dot(a, w, preferred_element_type=F32)
        acc = p if acc is None else acc + p
    if has_res:
        res_ref, gate_ref = refs[2 * n_terms], refs[2 * n_terms + 1]
        acc = res_ref[0] + gate_ref[0] * acc
    o_ref[0] = acc.astype(o_ref.dtype)


def _matmul(a_list, w_list, out_dtype, *, batch, rows, tm, tn, res=None, gate=None, name="matmul"):
    n = w_list[0].shape[-1]
    tm = min(tm, rows)
    tn = min(tn, n)
    a_batched = tuple(a.ndim == 3 for a in a_list)
    w_batched = tuple(w.ndim == 3 for w in w_list)
    in_specs = []
    for a, ab in zip(a_list, a_batched):
        k = a.shape[-1]
        in_specs.append(pl.BlockSpec((1, tm, k), lambda b, i, j: (b, i, 0)) if ab
                        else pl.BlockSpec((tm, k), lambda b, i, j: (i, 0)))
    for w, wb in zip(w_list, w_batched):
        k = w.shape[-2]
        in_specs.append(pl.BlockSpec((1, k, tn), lambda b, i, j: (b, 0, j)) if wb
                        else pl.BlockSpec((k, tn), lambda b, i, j: (0, j)))
    args = list(a_list) + list(w_list)
    if res is not None:
        in_specs.append(pl.BlockSpec((1, tm, tn), lambda b, i, j: (b, i, j)))
        in_specs.append(pl.BlockSpec((1, 1, tn), lambda b, i, j: (b, 0, j)))
        args += [res, gate]
    kern = functools.partial(_mm_kernel, n_terms=len(a_list), a_batched=a_batched, w_batched=w_batched,
                             has_res=res is not None)
    return pl.pallas_call(
        kern,
        grid=(batch, rows // tm, n // tn),
        in_specs=in_specs,
        out_specs=pl.BlockSpec((1, tm, tn), lambda b, i, j: (b, i, j)),
        out_shape=jax.ShapeDtypeStruct((batch, rows, n), out_dtype),
        compiler_params=_params(("parallel", "parallel", "parallel")),
        name=name,
    )(*args)


def _pair_ones():
    r = lax.broadcasted_iota(jnp.int32, (LANES, LANES), 0) // RWKV_HEAD
    c = lax.broadcasted_iota(jnp.int32, (LANES, LANES), 1) // RWKV_HEAD
    return jnp.where(r == c, 1.0, 0.0).astype(BF16)


def _head_sum(x, ones_bf16):
    return _dot_exact_rhs(x, ones_bf16)


def _token_shift(ref, s, n, t, mu_p, mu_n):
    x = ref[0, s:s + n, :]
    w = x.shape[-1]
    ridx = lax.broadcasted_iota(jnp.int32, (n, w), 0)
    prev_row = ref[0, s - 1:s, :] if s > 0 else jnp.zeros((1, w), F32)
    next_row = ref[0, s + n:s + n + 1, :] if s + n < t else jnp.zeros((1, w), F32)
    prev = jnp.where(ridx == 0, prev_row, pltpu.roll(x, 1, 0))
    nxt = jnp.where(ridx == n - 1, next_row, pltpu.roll(x, n - 1, 0))
    return x + mu_p * (prev - x) + mu_n * (nxt - x)


def _rwkv_prep_kernel(pr_ref, pk_ref, pv_ref, pl_ref, par_ref, lpar_ref, w2_ref, a2_ref, g2_ref,
                      r_ref, v_ref, kk_ref, g_ref, ks_ref, lw_ref, kd_ref, alr_ref, *, t, rows):
    ones = _pair_ones()
    par = par_ref[...]
    lpar = lpar_ref[...]
    k_k, k_a = par[10:11], par[11:12]
    for s in range(0, t, rows):
        sl = slice(s, s + rows)
        r = _token_shift(pr_ref, s, rows, t, par[0:1], par[1:2])
        k = _token_shift(pk_ref, s, rows, t, par[2:3], par[3:4])
        v = _token_shift(pv_ref, s, rows, t, par[4:5], par[5:6])
        lo = _token_shift(pl_ref, s, rows, t, lpar[0:1], lpar[1:2])
        xwa, xg = lo[:, :LANES], lo[:, LANES:]
        r_ref[0, sl, :] = r
        v_ref[0, sl, :] = v
        g_ref[0, sl, :] = _dot3(_sigmoid(xg), g2_ref[...])
        kk = k * k_k
        kk = kk * lax.rsqrt(_head_sum(kk * kk, ones) + KK_EPS)
        kk_ref[0, sl, :] = kk
        tw = jnp.tanh(xwa)
        xa = xwa
        ksum = None
        for d in range(2):
            z = -(par[6 + d:7 + d] + _dot3(tw, w2_ref[d]))
            softplus = jnp.maximum(z, 0.0) + jnp.log(1.0 + jnp.exp(-jnp.abs(z)))
            w_log = -softplus - 0.5
            lw_ref[d, 0, sl, :] = -jnp.exp(w_log)
            a = _sigmoid(par[8 + d:9 + d] + _dot3(xa, a2_ref[d]))
            alr_ref[d, 0, sl, :] = a
            k_d = k * (1.0 + (a - 1.0) * k_a)
            kd_ref[d, 0, sl, :] = k_d
            ksum = k_d if ksum is None else ksum + k_d
        ks_ref[0, sl, :] = ksum


def _rwkv_prep(pr, pk, pv, plo, par, lpar, w2, a2, g2):
    b, t, _ = pr.shape
    npair = D_A // LANES
    rows = min(256, t)
    lora = plo.shape[-1]
    slab = lambda: pl.BlockSpec((1, t, LANES), lambda bi, p: (bi, 0, p))
    dslab = lambda: pl.BlockSpec((2, 1, t, LANES), lambda bi, p: (0, bi, 0, p))
    one = jax.ShapeDtypeStruct((b, t, D_A), F32)
    two = jax.ShapeDtypeStruct((2, b, t, D_A), F32)
    return pl.pallas_call(
        functools.partial(_rwkv_prep_kernel, t=t, rows=rows),
        grid=(b, npair),
        in_specs=[slab(), slab(), slab(),
                  pl.BlockSpec((1, t, lora), lambda bi, p: (bi, 0, 0)),
                  pl.BlockSpec((16, LANES), lambda bi, p: (0, p)),
                  pl.BlockSpec((8, lora), lambda bi, p: (0, 0)),
                  pl.BlockSpec((2, LANES, LANES), lambda bi, p: (0, 0, p)),
                  pl.BlockSpec((2, LANES, LANES), lambda bi, p: (0, 0, p)),
                  pl.BlockSpec((G_LORA, LANES), lambda bi, p: (0, p))],
        out_specs=[slab(), slab(), slab(), slab(), slab(), dslab(), dslab(), dslab()],
        out_shape=[one, one, one, one, one, two, two, two],
        compiler_params=_params(("parallel", "parallel")),
        name="rwkv_prep",
    )(pr, pk, pv, plo, par, lpar, w2, a2, g2)


def _wkv_chunk(st, r, lw, k, v, kk, alr, sgn, consts):
    n = WKV_CHUNK
    n2 = 2 * n
    tri, keep, dt, eye, lvl = consts
    order = sgn * dt
    strict = order > 0
    incl = order >= 0
    cum = _dot_exact_lhs(jnp.where(sgn * tri >= 0, 1.0, 0.0).astype(BF16), lw)
    ctot = jnp.sum(lw, axis=0, keepdims=True)
    e_neg = jnp.exp(-cum)
    e_rem = jnp.exp(ctot - cum)
    a = -kk
    b = kk * alr

    def stack(x):
        return jnp.where(keep, jnp.concatenate([x, x], axis=0), 0.0)

    rt2 = stack(r * jnp.exp(cum))
    at2 = stack(a * jnp.exp(cum - lw))
    kt2 = stack(k * e_neg)
    bt2 = stack(b * e_neg)
    kh2 = stack(k * e_rem)
    bh2 = stack(b * e_rem)
    v2 = stack(v)
    g = _dot3(jnp.concatenate([at2, rt2], axis=0), jnp.concatenate([bt2, kt2], axis=0), NT_DIMS)
    a_ab = jnp.where(strict, g[:n2, :n2], 0.0)
    a_ak = jnp.where(strict, g[:n2, n2:], 0.0)
    a_rb = jnp.where(incl, g[n2:, :n2], 0.0)
    a_rk = jnp.where(incl, g[n2:, n2:], 0.0)
    tinv = eye + jnp.where(sgn * lvl[0] == 1, a_ab, 0.0)
    for m in lvl[1:]:
        cm = jnp.where(sgn * m == 1, a_ab, 0.0)
        tinv = tinv + _dot3(tinv, _dot3(cm, tinv))
    akv_rkv = _bdot(jnp.concatenate([a_ak, a_rk], axis=0), v2)
    wu = _dot3(tinv, jnp.concatenate([at2, akv_rkv[:n2]], axis=1))
    ry = jnp.concatenate([rt2, akv_rkv[n2:]], axis=1) + _bdot(a_rb, wu)
    bw = _bdot(bh2, wu, TN_DIMS)
    kv = _bdot(kh2, v2, TN_DIMS)
    y2 = _dot3(ry[:, :LANES], st) + ry[:, LANES:]
    decay_col = jnp.sum(jnp.where(eye > 0, jnp.exp(ctot), 0.0), axis=1, keepdims=True)
    st_new = decay_col * st + _dot3(bw[:, :LANES], st) + (bw[:, LANES:] + kv)
    return y2[:n] + y2[n:], st_new


def _wkv_consts():
    n = WKV_CHUNK
    n2 = 2 * n
    tri = lax.broadcasted_iota(jnp.int32, (n, n), 0) - lax.broadcasted_iota(jnp.int32, (n, n), 1)
    lane = lax.broadcasted_iota(jnp.int32, (n2, LANES), 1)
    rr = lax.broadcasted_iota(jnp.int32, (n2, LANES), 0)
    keep = (lane >= RWKV_HEAD) == (rr >= n)
    ri = lax.broadcasted_iota(jnp.int32, (n2, n2), 0)
    ci = lax.broadcasted_iota(jnp.int32, (n2, n2), 1)
    r2 = ri & (n - 1)
    c2 = ci & (n - 1)
    dt = r2 - c2
    eye = jnp.where(ri == ci, 1.0, 0.0).astype(F32)
    lvl = []
    k = 0
    while (1 << k) < n:
        same = (r2 >> (k + 1)) == (c2 >> (k + 1))
        diff = ((r2 >> k) & 1) - ((c2 >> k) & 1)
        lvl.append(jnp.where(same, diff, 0))
        k += 1
    return tri, keep, dt, eye, lvl


def _wkv_kernel(r_ref, v_ref, kk_ref, lw_ref, kd_ref, alr_ref, s0_ref, y_ref, sfin_ref, st_ref, *, n_sub):
    d = pl.program_id(0)
    c = pl.program_id(3)
    nc = pl.num_programs(3)

    @pl.when(c == 0)
    def _():
        st_ref[...] = s0_ref[0, 0, 0]

    sgn = 1 - 2 * d
    consts = _wkv_consts()
    st = st_ref[...]
    for u in range(n_sub):
        ui = u + d * (n_sub - 1 - 2 * u)
        rows = pl.ds(pl.multiple_of(ui * WKV_CHUNK, WKV_CHUNK), WKV_CHUNK)
        y, st = _wkv_chunk(st, r_ref[0, rows, :], lw_ref[0, 0, rows, :], kd_ref[0, 0, rows, :], v_ref[0, rows, :],
                           kk_ref[0, rows, :], alr_ref[0, 0, rows, :], sgn, consts)
        y_ref[0, 0, rows, :] = y
    st_ref[...] = st

    @pl.when(c == nc - 1)
    def _():
        sfin_ref[0, 0, 0] = st


def _wkv(r, v, kk, lw, kd, alr, s0):
    b, t, _ = r.shape
    npair = D_A // LANES
    n_sub = WKV_SUB
    blk = n_sub * WKV_CHUNK
    nc = t // blk

    def tmap(d, bi, p, c):
        return (bi, c + d * (nc - 1 - 2 * c), p)

    def dmap(d, bi, p, c):
        return (d, bi, c + d * (nc - 1 - 2 * c), p)

    smap = lambda d, bi, p, c: (d, bi, p, 0, 0)
    shared = lambda: pl.BlockSpec((1, blk, LANES), tmap)
    direc = lambda: pl.BlockSpec((1, 1, blk, LANES), dmap)
    state = lambda: pl.BlockSpec((1, 1, 1, LANES, LANES), smap)
    return pl.pallas_call(
        functools.partial(_wkv_kernel, n_sub=n_sub),
        grid=(2, b, npair, nc),
        in_specs=[shared(), shared(), shared(), direc(), direc(), direc(), state()],
        out_specs=[direc(), state()],
        out_shape=[jax.ShapeDtypeStruct((2, b, t, D_A), F32),
                   jax.ShapeDtypeStruct((2, b, npair, LANES, LANES), F32)],
        scratch_shapes=[pltpu.VMEM((LANES, LANES), F32)],
        compiler_params=_params(("parallel", "parallel", "parallel", "arbitrary")),
        name="wkv",
    )(r, v, kk, lw, kd, alr, s0)


def _rwkv_finish_kernel(y_ref, r_ref, v_ref, ks_ref, g_ref, par_ref, o_ref):
    ones = _pair_ones()
    par = par_ref[...]
    inv_n = 1.0 / RWKV_HEAD
    for p in range(D_A // LANES):
        sl = slice(p * LANES, (p + 1) * LANES)
        y = y_ref[0, 0, :, sl] + y_ref[1, 0, :, sl]
        mu = _head_sum(y, ones) * inv_n
        yc = y - mu
        var = _head_sum(yc * yc, ones) * inv_n
        yn = yc * lax.rsqrt(var + LNX_EPS) * par[0:1, sl] + par[1:2, sl]
        bonus = _head_sum(r_ref[0, :, sl] * ks_ref[0, :, sl] * par[2:3, sl], ones) * v_ref[0, :, sl]
        o_ref[0, :, sl] = ((yn + bonus) * g_ref[0, :, sl]).astype(BF16)


def _rwkv_finish(y, r, v, ks, g, par, tm=256):
    b, t, d = r.shape
    tile = lambda: pl.BlockSpec((1, tm, d), lambda bi, i: (bi, i, 0))
    return pl.pallas_call(
        _rwkv_finish_kernel,
        grid=(b, t // tm),
        in_specs=[pl.BlockSpec((2, 1, tm, d), lambda bi, i: (0, bi, i, 0)), tile(), tile(), tile(), tile(),
                  pl.BlockSpec((8, d), lambda bi, i: (0, 0))],
        out_specs=tile(),
        out_shape=jax.ShapeDtypeStruct((b, t, d), BF16),
        compiler_params=_params(("parallel", "parallel")),
        name="rwkv_finish",
    )(y, r, v, ks, g, par)


def _head_rms(x, gain):
    return x * lax.rsqrt(jnp.mean(x * x, axis=-1, keepdims=True) + NORM_EPS) * gain


def _rope(x, cos_e, sin_s):
    lane = lax.broadcasted_iota(jnp.int32, x.shape, 1)
    swapped = jnp.where((lane & 1) == 0, pltpu.roll(x, LANES - 1, 1), pltpu.roll(x, 1, 1))
    return x * cos_e + swapped * sin_s


def _attn_prep_kernel(*refs, with_q):
    if with_q:
        pq_ref, pkv_ref, gains_ref, cos_ref, sin_ref, q_ref, k_ref, v_ref = refs
    else:
        pkv_ref, gains_ref, k_ref, v_ref = refs
    gains = gains_ref[...]
    kv_w = H_KV * HEAD_DIM
    for h in range(H_KV):
        sl = slice(h * HEAD_DIM, (h + 1) * HEAD_DIM)
        kh = _head_rms(pkv_ref[0, :, sl], gains[1:2])
        if with_q:
            kh = _rope(kh, cos_ref[...], sin_ref[...])
        k_ref[0, :, sl] = kh.astype(BF16)
    v_ref[0] = pkv_ref[0, :, kv_w:].astype(BF16)
    if with_q:
        scale = HEAD_DIM ** -0.5
        for h in range(H_Q):
            sl = slice(h * HEAD_DIM, (h + 1) * HEAD_DIM)
            qh = _rope(_head_rms(pq_ref[0, :, sl], gains[0:1]), cos_ref[...], sin_ref[...])
            q_ref[0, :, sl] = (qh * scale).astype(BF16)


def _attn_prep(pq, pkv, gains, cos_e, sin_s, tm=256):
    b, t, _ = pkv.shape
    with_q = pq is not None
    kv_w = H_KV * HEAD_DIM
    tm = min(tm, t)
    kv_spec = lambda: pl.BlockSpec((1, tm, kv_w), lambda bi, i: (bi, i, 0))
    in_specs = [pl.BlockSpec((1, tm, 2 * kv_w), lambda bi, i: (bi, i, 0)),
                pl.BlockSpec((8, HEAD_DIM), lambda bi, i: (0, 0))]
    args = [pkv, gains]
    out_specs = [kv_spec(), kv_spec()]
    out_shape = [jax.ShapeDtypeStruct((b, t, kv_w), BF16)] * 2
    if with_q:
        in_specs = [pl.BlockSpec((1, tm, D_B), lambda bi, i: (bi, i, 0))] + in_specs + [
            pl.BlockSpec((tm, HEAD_DIM), lambda bi, i: (i, 0)),
            pl.BlockSpec((tm, HEAD_DIM), lambda bi, i: (i, 0))]
        args = [pq] + args + [cos_e, sin_s]
        out_specs = [pl.BlockSpec((1, tm, D_B), lambda bi, i: (bi, i, 0))] + out_specs
        out_shape = [jax.ShapeDtypeStruct((b, t, D_B), BF16)] + out_shape
    return pl.pallas_call(
        functools.partial(_attn_prep_kernel, with_q=with_q),
        grid=(b, t // tm),
        in_specs=in_specs,
        out_specs=out_specs,
        out_shape=out_shape,
        compiler_params=_params(("parallel", "parallel")),
        name="attn_prep",
    )(*args)


def _attn_kernel(q_ref, k_ref, v_ref, o_ref):
    k = k_ref[0]
    v = v_ref[0]
    for g in range(Q_PER_KV):
        sl = slice(g * HEAD_DIM, (g + 1) * HEAD_DIM)
        s = lax.dot_general(q_ref[0, :, sl], k, NT_DIMS, preferred_element_type=F32)
        p = jnp.exp(s - jnp.max(s, axis=-1, keepdims=True))
        l = jnp.sum(p, axis=-1, keepdims=True)
        o = jnp.dot(p.astype(BF16), v, preferred_element_type=F32)
        o_ref[0, :, sl] = (o / l).astype(BF16)


def _attention(q, k_all, v_all, tq=256):
    b, t, _ = q.shape
    s_len = k_all.shape[1]
    gw = Q_PER_KV * HEAD_DIM
    return pl.pallas_call(
        _attn_kernel,
        grid=(b, H_KV, t // tq),
        in_specs=[pl.BlockSpec((1, tq, gw), lambda bi, h, i: (bi, i, h)),
                  pl.BlockSpec((1, s_len, HEAD_DIM), lambda bi, h, i: (bi, 0, h)),
                  pl.BlockSpec((1, s_len, HEAD_DIM), lambda bi, h, i: (bi, 0, h))],
        out_specs=pl.BlockSpec((1, tq, gw), lambda bi, h, i: (bi, i, h)),
        out_shape=jax.ShapeDtypeStruct((b, t, H_Q * HEAD_DIM), BF16),
        compiler_params=_params(("parallel", "parallel", "parallel")),
        name="attention",
    )(q, k_all, v_all)


def _dft_channels_kernel(h_ref, w_ref, c_ref, s_ref):
    p = jnp.dot(h_ref[0], w_ref[...], preferred_element_type=F32)
    c_ref[0] = p[:, :FOURIER_GROUP_DIM].astype(BF16)
    s_ref[0] = p[:, FOURIER_GROUP_DIM:].astype(BF16)


def _dft_channels(h, w_cs, tm=512):
    b, t, d = h.shape
    tm = min(tm, t)
    gd = FOURIER_GROUP_DIM
    blk = lambda: pl.BlockSpec((1, tm, gd), lambda bi, i, g: (bi, i, g))
    return pl.pallas_call(
        _dft_channels_kernel,
        grid=(b, t // tm, d // gd),
        in_specs=[blk(), pl.BlockSpec((gd, 2 * gd), lambda bi, i, g: (0, 0))],
        out_specs=[blk(), blk()],
        out_shape=[jax.ShapeDtypeStruct((b, t, d), BF16)] * 2,
        compiler_params=_params(("parallel", "parallel", "parallel")),
        name="dft_channels",
    )(h, w_cs)


def _dft_mats(n):
    i = lax.broadcasted_iota(jnp.int32, (n, n), 0)
    j = lax.broadcasted_iota(jnp.int32, (n, n), 1)
    ang = ((i * j) % n).astype(F32) * (2.0 * np.pi / n)
    scale = n ** -0.5
    return jnp.cos(ang) * scale, jnp.sin(ang) * scale


def _moe_kernel(h_ref, gates_ref, wg_ref, wu_ref, wd_ref, res_ref, gate_ref, o_ref, acc_ref):
    e = pl.program_id(2)
    ne = pl.num_programs(2)

    @pl.when(e == 0)
    def _():
        acc_ref[...] = jnp.zeros_like(acc_ref)

    h = h_ref[0]
    u = _silu(jnp.dot(h, wg_ref[0], preferred_element_type=F32)) * jnp.dot(h, wu_ref[0], preferred_element_type=F32)
    lane = lax.broadcasted_iota(jnp.int32, gates_ref.shape[1:], 1)
    gcol = jnp.sum(jnp.where(lane == e, gates_ref[0], 0.0), axis=1, keepdims=True)
    acc_ref[...] += gcol * jnp.dot(u.astype(BF16), wd_ref[0], preferred_element_type=F32)

    @pl.when(e == ne - 1)
    def _():
        o_ref[0] = res_ref[0] + gate_ref[0] * acc_ref[...]


def _moe(h, gates, wg, wu, wd, res, gate, tm=512):
    b, t, d = h.shape
    tm = min(tm, t)
    ne, _, de = wg.shape
    tok = lambda: pl.BlockSpec((1, tm, d), lambda bi, i, e: (bi, i, 0))
    return pl.pallas_call(
        _moe_kernel,
        grid=(b, t // tm, ne),
        in_specs=[tok(),
                  pl.BlockSpec((1, tm, ne), lambda bi, i, e: (bi, i, 0)),
                  pl.BlockSpec((1, d, de), lambda bi, i, e: (e, 0, 0)),
                  pl.BlockSpec((1, d, de), lambda bi, i, e: (e, 0, 0)),
                  pl.BlockSpec((1, de, d), lambda bi, i, e: (e, 0, 0)),
                  tok(),
                  pl.BlockSpec((1, 1, d), lambda bi, i, e: (bi, 0, 0))],
        out_specs=tok(),
        out_shape=jax.ShapeDtypeStruct((b, t, d), F32),
        scratch_shapes=[pltpu.VMEM((tm, d), F32)],
        compiler_params=_params(("parallel", "parallel", "arbitrary")),
        name="moe",
    )(h, gates, wg, wu, wd, res, gate)


def _rope_tables(n):
    rows = n // GRID_W
    row = jnp.repeat(jnp.arange(rows), GRID_W).astype(F32)
    col = jnp.tile(jnp.arange(GRID_W), rows).astype(F32)
    inv = ROPE_THETA ** (-jnp.arange(0, ROPE_AXIS_DIM, 2, dtype=F32) / ROPE_AXIS_DIM)
    ang = jnp.concatenate([row[:, None] * inv, col[:, None] * inv], axis=-1)
    cos_e = jnp.repeat(jnp.cos(ang), 2, axis=-1)
    sin_s = jnp.stack([-jnp.sin(ang), jnp.sin(ang)], axis=-1).reshape(n, HEAD_DIM)
    return cos_e, sin_s


def _pad_rows(rows, n):
    a = jnp.stack(rows, axis=0)
    return jnp.pad(a, ((0, n - a.shape[0]), (0, 0)))


def _even_mixer(x, ctx, mods, gain, w_in, mu_prev, mu_next, w0, w2, a0, a2, g2, k_k, k_a, r_k, ln_g, ln_b,
                q_gain, k_gain, w_out):
    bsz, seq, _ = x.shape
    sh1, sc1, gt1, csh1, csc1 = mods
    h_lat = _norm_mod(x, gain, sc1, sh1)
    h_ctx = _norm_mod(ctx, gain, csc1, csh1)

    a_cols = 3 * D_A + W_LORA + A_LORA + G_LORA
    wb = w_in.astype(BF16)
    w_r, w_k, w_v = wb[:, :D_A], wb[:, D_A:2 * D_A], wb[:, 2 * D_A:3 * D_A]
    w_lo = wb[:, 3 * D_A:a_cols]
    w_q = wb[:, a_cols:a_cols + D_B]
    w_kv = wb[:, a_cols + D_B:]

    def proj(h, w, name):
        rows = h.shape[1]
        return _matmul([h], [w], F32, batch=bsz, rows=rows, tm=1024, tn=512, name=name)

    par = _pad_rows([mu_prev[:D_A], mu_next[:D_A], mu_prev[D_A:2 * D_A], mu_next[D_A:2 * D_A],
                     mu_prev[2 * D_A:3 * D_A], mu_next[2 * D_A:3 * D_A],
                     w0[0], w0[1], a0[0], a0[1], k_k, k_a], 16)
    lpar = _pad_rows([mu_prev[3 * D_A:], mu_next[3 * D_A:]], 8)
    fin_par = _pad_rows([ln_g, ln_b, r_k.reshape(-1)], 8)
    w2 = jnp.pad(w2, ((0, 0), (0, A_LORA), (0, 0)))
    a2 = jnp.pad(a2, ((0, 0), (W_LORA, 0), (0, 0)))

    def streams(h):
        return _rwkv_prep(proj(h, w_r, "proj_r"), proj(h, w_k, "proj_k"), proj(h, w_v, "proj_v"),
                          proj(h, w_lo, "proj_lora"), par, lpar, w2, a2, g2)

    rc, vc, kkc, _, _, lwc, kdc, alrc = streams(h_ctx)
    rl, vl, kkl, gl, ksl, lwl, kdl, alrl = streams(h_lat)
    s0 = jnp.zeros((2, bsz, D_A // LANES, LANES, LANES), F32)
    _, s_ctx = _wkv(rc, vc, kkc, lwc, kdc, alrc, s0)
    y, _ = _wkv(rl, vl, kkl, lwl, kdl, alrl, s_ctx)
    ya = _rwkv_finish(y, rl, vl, ksl, gl, fin_par)

    cos_e, sin_s = _rope_tables(seq)
    gains = _pad_rows([q_gain, k_gain], 8)
    k_c, v_c = _attn_prep(None, proj(h_ctx, w_kv, "proj_kv"), gains, None, None)
    q_l, k_l, v_l = _attn_prep(proj(h_lat, w_q, "proj_q"), proj(h_lat, w_kv, "proj_kv"), gains, cos_e, sin_s)
    yb = _attention(q_l, jnp.concatenate([k_c, k_l], axis=1), jnp.concatenate([v_c, v_l], axis=1))

    wo = w_out.astype(BF16)
    return _matmul([ya, yb], [wo[:D_A], wo[D_A:]], F32, batch=bsz, rows=seq, tm=1024, tn=512,
                   res=x, gate=gt1, name="mixer_out")


def _fourier_mixer(x, mods, gain, w_out):
    bsz, seq, d = x.shape
    sh1, sc1, gt1 = mods
    h = _norm_mod(x, gain, sc1, sh1)
    cc, sc = _dft_mats(FOURIER_GROUP_DIM)
    ct, st = _dft_mats(seq)
    pc, ps = _dft_channels(h, jnp.concatenate([cc, sc], axis=1).astype(BF16))
    f = _matmul([ct.astype(BF16), (-st).astype(BF16)], [pc, ps], BF16, batch=bsz, rows=seq, tm=512, tn=512,
                name="dft_tokens")
    return _matmul([f], [w_out.astype(BF16)], F32, batch=bsz, rows=seq, tm=1024, tn=512, res=x, gate=gt1,
                   name="mixer_out")


def _moe_layer(x, mods, gain, router_w_t, router_b, w_gate, w_up, w_down):
    sh2, sc2, gt2 = mods
    tokens, gates_t = _norm_mod_router(x, gain, sc2, sh2, router_w_t, router_b)
    gates = jnp.swapaxes(gates_t, 1, 2)
    return _moe(tokens, gates, w_gate.astype(BF16), w_up.astype(BF16), w_down.astype(BF16), x, gt2)


def kernel(x, c, ctx, c_ctx, mod_w, mod_b, norm_mix, norm_ffn, ab_w_in, ab_mu_prev, ab_mu_next,
           rwkv_w0, rwkv_w2, rwkv_a0, rwkv_a2, rwkv_g2, rwkv_k_k, rwkv_k_a, rwkv_r_k, rwkv_ln_g, rwkv_ln_b,
           attn_q_norm, attn_k_norm, ab_w_out, fourier_w_out, router_w, router_b,
           exp_w_gate, exp_w_up, exp_w_down):
    bsz, seq, d = x.shape
    depth = mod_w.shape[0]
    cvec = jnp.concatenate([c, c_ctx[None], jnp.zeros((8 - bsz - 1, d), F32)], axis=0)
    mod = _modulation(cvec, mod_w, mod_b)
    router_w_t = router_w.T
    for l in range(depth):
        m_lat = mod[l, :bsz, None, :]
        sh1, sc1, gt1, sh2, sc2, gt2 = jnp.split(m_lat, 6, axis=-1)
        if l % 2 == 0:
            e = l // 2
            m_ctx = mod[l, bsz:bsz + 1, None, :]
            csh1, csc1 = m_ctx[..., :d], m_ctx[..., d:2 * d]
            x = _even_mixer(x, ctx, (sh1, sc1, gt1, csh1, csc1), norm_mix[l], ab_w_in[e], ab_mu_prev[e],
                            ab_mu_next[e], rwkv_w0[e], rwkv_w2[e], rwkv_a0[e], rwkv_a2[e], rwkv_g2[e],
                            rwkv_k_k[e], rwkv_k_a[e], rwkv_r_k[e], rwkv_ln_g[e], rwkv_ln_b[e],
                            attn_q_norm[e], attn_k_norm[e], ab_w_out[e])
        else:
            x = _fourier_mixer(x, (sh1, sc1, gt1), norm_mix[l], fourier_w_out[l // 2])
        x = _moe_layer(x, (sh2, sc2, gt2), norm_ffn[l], router_w_t, router_b,
                       exp_w_gate[l], exp_w_up[l], exp_w_down[l])
    return x
```

```python
import functools

import jax
import jax.numpy as jnp
import numpy as np
from jax import lax
from jax.experimental import pallas as pl
from jax.experimental.pallas import tpu as pltpu

F32 = jnp.float32
BF16 = jnp.bfloat16

D_MODEL = 2048
D_A = 1024
RWKV_HEAD = 64
W_LORA = 64
A_LORA = 64
G_LORA = 128
LNX_EPS = 64e-5
D_B = 1024
HEAD_DIM = 128
H_Q = 8
H_KV = 2
Q_PER_KV = H_Q // H_KV
GRID_W = 64
ROPE_THETA = 10000.0
ROPE_AXIS_DIM = HEAD_DIM // 2
FOURIER_GROUPS = 8
FOURIER_GROUP_DIM = D_MODEL // FOURIER_GROUPS
N_EXPERTS = 16
N_EXPERT_GROUPS = 4
EXPERTS_PER_GROUP = 4
D_EXPERT = 512
NORM_EPS = 1e-6
KK_EPS = 1e-12

LANES = 128
WKV_CHUNK = 64
WKV_SUB = 8
VMEM_LIMIT = 56 * 1024 * 1024

NT_DIMS = (((1,), (1,)), ((), ()))
TN_DIMS = (((0,), (0,)), ((), ()))


def _params(sem):
    return pltpu.CompilerParams(dimension_semantics=sem, vmem_limit_bytes=VMEM_LIMIT)


def _bdot(a, b, dims=None):
    a = a.astype(BF16)
    b = b.astype(BF16)
    if dims is None:
        return jnp.dot(a, b, preferred_element_type=F32)
    return lax.dot_general(a, b, dims, preferred_element_type=F32)


def _split2(x):
    hi = x.astype(BF16)
    lo = (x - hi.astype(F32)).astype(BF16)
    return hi, lo


def _split3(x):
    hi = x.astype(BF16)
    r1 = x - hi.astype(F32)
    mid = r1.astype(BF16)
    lo = (r1 - mid.astype(F32)).astype(BF16)
    return hi, mid, lo


def _dot3(a, b, dims=None):
    ah, al = _split2(a)
    bh, bl = _split2(b)
    return _bdot(ah, bh, dims) + (_bdot(ah, bl, dims) + _bdot(al, bh, dims))


def _dot_exact_lhs(a_bf16, x, dims=None):
    h, m, l = _split3(x)
    return _bdot(a_bf16, h, dims) + (_bdot(a_bf16, m, dims) + _bdot(a_bf16, l, dims))


def _dot_exact_rhs(x, b_bf16):
    h, m, l = _split3(x)
    return _bdot(h, b_bf16) + (_bdot(m, b_bf16) + _bdot(l, b_bf16))


def _sigmoid(x):
    return 1.0 / (1.0 + jnp.exp(-x))


def _silu(x):
    return x * _sigmoid(x)


def _mod_kernel(c_ref, w_ref, b_ref, o_ref):
    s = _silu(c_ref[...])
    o_ref[0] = _bdot(s, w_ref[0]) + b_ref[0]


def _modulation(cvec, mod_w, mod_b):
    depth, d, n = mod_w.shape
    tn = 1024
    return pl.pallas_call(
        _mod_kernel,
        grid=(depth, n // tn),
        in_specs=[pl.BlockSpec((8, d), lambda l, j: (0, 0)),
                  pl.BlockSpec((1, d, tn), lambda l, j: (l, 0, j)),
                  pl.BlockSpec((1, 1, tn), lambda l, j: (l, 0, j))],
        out_specs=pl.BlockSpec((1, 8, tn), lambda l, j: (l, 0, j)),
        out_shape=jax.ShapeDtypeStruct((depth, 8, n), F32),
        compiler_params=_params(("parallel", "parallel")),
        name="modulation",
    )(cvec, mod_w, mod_b.reshape(depth, 1, n))


def _normed(x, gain, sc, sh):
    y = x * lax.rsqrt(jnp.mean(x * x, axis=-1, keepdims=True) + NORM_EPS)
    return (y * gain) * (1.0 + sc) + sh


def _norm_mod_kernel(x_ref, g_ref, sc_ref, sh_ref, o_ref):
    o_ref[0] = _normed(x_ref[0], g_ref[...], sc_ref[0], sh_ref[0]).astype(BF16)


def _top2_gates(logits, bias):
    s = _sigmoid(logits)
    sel = s + bias
    neg = jnp.float32(-jnp.inf)
    flags, scores = [], []
    for g in range(N_EXPERT_GROUPS):
        xs = [sel[g * EXPERTS_PER_GROUP + i:g * EXPERTS_PER_GROUP + i + 1] for i in range(EXPERTS_PER_GROUP)]
        m1 = functools.reduce(jnp.maximum, xs)
        first, taken = [], None
        for x in xs:
            f = (x == m1) if taken is None else jnp.logical_and(x == m1, jnp.logical_not(taken))
            taken = f if taken is None else jnp.logical_or(taken, f)
            first.append(f)
        rest = [jnp.where(f, neg, x) for f, x in zip(first, xs)]
        m2 = functools.reduce(jnp.maximum, rest)
        second, taken = [], None
        for f, x in zip(first, xs):
            c = jnp.logical_and(x == m2, jnp.logical_not(f))
            if taken is not None:
                c = jnp.logical_and(c, jnp.logical_not(taken))
            taken = c if taken is None else jnp.logical_or(taken, c)
            second.append(c)
        flags.append([jnp.logical_or(f, c) for f, c in zip(first, second)])
        scores.append(m1 + m2)
    best = functools.reduce(jnp.maximum, scores)
    taken, rows = None, []
    for g in range(N_EXPERT_GROUPS):
        bg = (scores[g] == best) if taken is None else jnp.logical_and(scores[g] == best, jnp.logical_not(taken))
        taken = bg if taken is None else jnp.logical_or(taken, bg)
        for i in range(EXPERTS_PER_GROUP):
            e = g * EXPERTS_PER_GROUP + i
            rows.append(jnp.where(jnp.logical_and(bg, flags[g][i]), s[e:e + 1], 0.0))
    denom = functools.reduce(jnp.add, rows)
    inv = 1.0 / denom
    return [r * inv for r in rows]


def _norm_mod_router_kernel(x_ref, g_ref, sc_ref, sh_ref, rw_ref, rb_ref, o_ref, gates_ref):
    h = _normed(x_ref[0], g_ref[...], sc_ref[0], sh_ref[0])
    o_ref[0] = h.astype(BF16)
    logits = _dot3(rw_ref[...], h, NT_DIMS)
    for e, row in enumerate(_top2_gates(logits, rb_ref[...])):
        gates_ref[0, e:e + 1, :] = row


def _norm_mod(x, gain, sc, sh, tm=512):
    b, t, d = x.shape
    tm = min(tm, t)
    per_b = sc.shape[0] > 1
    mod_map = (lambda bi, i: (bi, 0, 0)) if per_b else (lambda bi, i: (0, 0, 0))
    return pl.pallas_call(
        _norm_mod_kernel,
        grid=(b, t // tm),
        in_specs=[pl.BlockSpec((1, tm, d), lambda bi, i: (bi, i, 0)),
                  pl.BlockSpec((1, d), lambda bi, i: (0, 0)),
                  pl.BlockSpec((1, 1, d), mod_map),
                  pl.BlockSpec((1, 1, d), mod_map)],
        out_specs=pl.BlockSpec((1, tm, d), lambda bi, i: (bi, i, 0)),
        out_shape=jax.ShapeDtypeStruct((b, t, d), BF16),
        compiler_params=_params(("parallel", "parallel")),
        name="norm_mod",
    )(x, gain.reshape(1, d), sc, sh)


def _norm_mod_router(x, gain, sc, sh, router_w_t, router_b, tm=512):
    b, t, d = x.shape
    tm = min(tm, t)
    e = router_w_t.shape[0]
    return pl.pallas_call(
        _norm_mod_router_kernel,
        grid=(b, t // tm),
        in_specs=[pl.BlockSpec((1, tm, d), lambda bi, i: (bi, i, 0)),
                  pl.BlockSpec((1, d), lambda bi, i: (0, 0)),
                  pl.BlockSpec((1, 1, d), lambda bi, i: (bi, 0, 0)),
                  pl.BlockSpec((1, 1, d), lambda bi, i: (bi, 0, 0)),
                  pl.BlockSpec((e, d), lambda bi, i: (0, 0)),
                  pl.BlockSpec((e, 1), lambda bi, i: (0, 0))],
        out_specs=[pl.BlockSpec((1, tm, d), lambda bi, i: (bi, i, 0)),
                   pl.BlockSpec((1, e, tm), lambda bi, i: (bi, 0, i))],
        out_shape=[jax.ShapeDtypeStruct((b, t, d), BF16),
                   jax.ShapeDtypeStruct((b, e, t), F32)],
        compiler_params=_params(("parallel", "parallel")),
        name="norm_mod_router",
    )(x, gain.reshape(1, d), sc, sh, router_w_t, router_b.reshape(e, 1))


def _mm_kernel(*refs, n_terms, a_batched, w_batched, has_res):
    a_refs = refs[:n_terms]
    w_refs = refs[n_terms:2 * n_terms]
    o_ref = refs[-1]
    acc = None
    for a_ref, w_ref, ab, wb in zip(a_refs, w_refs, a_batched, w_batched):
        a = a_ref[0] if ab else a_ref[...]
        w = w_ref[0] if wb else w_ref[...]
        p = jnp.dot(a, w, preferred_element_type=F32)
        acc = p if acc is None else acc + p
    if has_res:
        res_ref, gate_ref = refs[2 * n_terms], refs[2 * n_terms + 1]
        acc = res_ref[0] + gate_ref[0] * acc
    o_ref[0] = acc.astype(o_ref.dtype)


def _matmul(a_list, w_list, out_dtype, *, batch, rows, tm, tn, res=None, gate=None, name="matmul"):
    n = w_list[0].shape[-1]
    tm = min(tm, rows)
    tn = min(tn, n)
    a_batched = tuple(a.ndim == 3 for a in a_list)
    w_batched = tuple(w.ndim == 3 for w in w_list)
    in_specs = []
    for a, ab in zip(a_list, a_batched):
        k = a.shape[-1]
        in_specs.append(pl.BlockSpec((1, tm, k), lambda b, i, j: (b, i, 0)) if ab
                        else pl.BlockSpec((tm, k), lambda b, i, j: (i, 0)))
    for w, wb in zip(w_list, w_batched):
        k = w.shape[-2]
        in_specs.append(pl.BlockSpec((1, k, tn), lambda b, i, j: (b, 0, j)) if wb
                        else pl.BlockSpec((k, tn), lambda b, i, j: (0, j)))
    args = list(a_list) + list(w_list)
    if res is not None:
        in_specs.append(pl.BlockSpec((1, tm, tn), lambda b, i, j: (b, i, j)))
        in_specs.append(pl.BlockSpec((1, 1, tn), lambda b, i, j: (b, 0, j)))
        args += [res, gate]
    kern = functools.partial(_mm_kernel, n_terms=len(a_list), a_batched=a_batched, w_batched=w_batched,
                             has_res=res is not None)
    return pl.pallas_call(
        kern,
        grid=(batch, rows // tm, n // tn),
        in_specs=in_specs,
        out_specs=pl.BlockSpec((1, tm, tn), lambda b, i, j: (b, i, j)),
        out_shape=jax.ShapeDtypeStruct((batch, rows, n), out_dtype),
        compiler_params=_params(("parallel", "parallel", "parallel")),
        name=name,
    )(*args)


def _pair_ones():
    r = lax.broadcasted_iota(jnp.int32, (LANES, LANES), 0) // RWKV_HEAD
    c = lax.broadcasted_iota(jnp.int32, (LANES, LANES), 1) // RWKV_HEAD
    return jnp.where(r == c, 1.0, 0.0).astype(BF16)


def _head_sum(x, ones_bf16):
    return _dot_exact_rhs(x, ones_bf16)


def _token_shift(ref, s, n, t, mu_p, mu_n):
    x = ref[0, s:s + n, :]
    w = x.shape[-1]
    ridx = lax.broadcasted_iota(jnp.int32, (n, w), 0)
    prev_row = ref[0, s - 1:s, :] if s > 0 else jnp.zeros((1, w), F32)
    next_row = ref[0, s + n:s + n + 1, :] if s + n < t else jnp.zeros((1, w), F32)
    prev = jnp.where(ridx == 0, prev_row, pltpu.roll(x, 1, 0))
    nxt = jnp.where(ridx == n - 1, next_row, pltpu.roll(x, n - 1, 0))
    return x + mu_p * (prev - x) + mu_n * (nxt - x)


def _rwkv_prep_kernel(pr_ref, pk_ref, pv_ref, pl_ref, par_ref, lpar_ref, w2_ref, a2_ref, g2_ref,
                      r_ref, v_ref, kk_ref, g_ref, ks_ref, lw_ref, kd_ref, alr_ref, *, t, rows):
    ones = _pair_ones()
    par = par_ref[...]
    lpar = lpar_ref[...]
    k_k, k_a = par[10:11], par[11:12]
    for s in range(0, t, rows):
        sl = slice(s, s + rows)
        r = _token_shift(pr_ref, s, rows, t, par[0:1], par[1:2])
        k = _token_shift(pk_ref, s, rows, t, par[2:3], par[3:4])
        v = _token_shift(pv_ref, s, rows, t, par[4:5], par[5:6])
        lo = _token_shift(pl_ref, s, rows, t, lpar[0:1], lpar[1:2])
        xwa, xg = lo[:, :LANES], lo[:, LANES:]
        r_ref[0, sl, :] = r
        v_ref[0, sl, :] = v
        g_ref[0, sl, :] = _dot3(_sigmoid(xg), g2_ref[...])
        kk = k * k_k
        kk = kk * lax.rsqrt(_head_sum(kk * kk, ones) + KK_EPS)
        kk_ref[0, sl, :] = kk
        tw = jnp.tanh(xwa)
        xa = xwa
        ksum = None
        for d in range(2):
            z = -(par[6 + d:7 + d] + _dot3(tw, w2_ref[d]))
            softplus = jnp.maximum(z, 0.0) + jnp.log(1.0 + jnp.exp(-jnp.abs(z)))
            w_log = -softplus - 0.5
            lw_ref[d, 0, sl, :] = -jnp.exp(w_log)
            a = _sigmoid(par[8 + d:9 + d] + _dot3(xa, a2_ref[d]))
            alr_ref[d, 0, sl, :] = a
            k_d = k * (1.0 + (a - 1.0) * k_a)
            kd_ref[d, 0, sl, :] = k_d
            ksum = k_d if ksum is None else ksum + k_d
        ks_ref[0, sl, :] = ksum


def _rwkv_prep(pr, pk, pv, plo, par, lpar, w2, a2, g2):
    b, t, _ = pr.shape
    npair = D_A // LANES
    rows = min(256, t)
    lora = plo.shape[-1]
    slab = lambda: pl.BlockSpec((1, t, LANES), lambda bi, p: (bi, 0, p))
    dslab = lambda: pl.BlockSpec((2, 1, t, LANES), lambda bi, p: (0, bi, 0, p))
    one = jax.ShapeDtypeStruct((b, t, D_A), F32)
    two = jax.ShapeDtypeStruct((2, b, t, D_A), F32)
    return pl.pallas_call(
        functools.partial(_rwkv_prep_kernel, t=t, rows=rows),
        grid=(b, npair),
        in_specs=[slab(), slab(), slab(),
                  pl.BlockSpec((1, t, lora), lambda bi, p: (bi, 0, 0)),
                  pl.BlockSpec((16, LANES), lambda bi, p: (0, p)),
                  pl.BlockSpec((8, lora), lambda bi, p: (0, 0)),
                  pl.BlockSpec((2, LANES, LANES), lambda bi, p: (0, 0, p)),
                  pl.BlockSpec((2, LANES, LANES), lambda bi, p: (0, 0, p)),
                  pl.BlockSpec((G_LORA, LANES), lambda bi, p: (0, p))],
        out_specs=[slab(), slab(), slab(), slab(), slab(), dslab(), dslab(), dslab()],
        out_shape=[one, one, one, one, one, two, two, two],
        compiler_params=_params(("parallel", "parallel")),
        name="rwkv_prep",
    )(pr, pk, pv, plo, par, lpar, w2, a2, g2)


_gram_dot = _bdot
_inv_dot = _bdot
_wu_dot = _bdot
_state_dot = _bdot


def _wkv_local(chunks, sgn, consts):
    n = WKV_CHUNK
    n2 = 2 * n
    tri, keep, dt, eye, lvl = consts
    order = sgn * dt
    strict = order > 0
    incl = order >= 0
    tri_b = jnp.where(sgn * tri >= 0, 1.0, 0.0).astype(BF16)
    each = lambda f, *ls: [f(*xs) for xs in zip(*ls)]

    def stack(x):
        return jnp.where(keep, jnp.concatenate([x, x], axis=0), 0.0)

    lws = [c[1] for c in chunks]
    cums = each(lambda lw: _dot_exact_lhs(tri_b, lw), lws)
    ctots = each(lambda lw: jnp.sum(lw, axis=0, keepdims=True), lws)
    lhs, rhs, kh2s, bh2s, v2s, at2s, rt2s = [], [], [], [], [], [], []
    for (r, lw, k, v, kk, alr), cum, ctot in zip(chunks, cums, ctots):
        e_neg = jnp.exp(-cum)
        e_rem = jnp.exp(ctot - cum)
        b = kk * alr
        rt2 = stack(r * jnp.exp(cum))
        at2 = stack(-kk * jnp.exp(cum - lw))
        lhs.append(jnp.concatenate([at2, rt2], axis=0))
        rhs.append(jnp.concatenate([stack(b * e_neg), stack(k * e_neg)], axis=0))
        kh2s.append(stack(k * e_rem))
        bh2s.append(stack(b * e_rem))
        v2s.append(stack(v))
        at2s.append(at2)
        rt2s.append(rt2)
    gs = each(lambda x, y: _gram_dot(x, y, NT_DIMS), lhs, rhs)
    a_abs = [jnp.where(strict, g[:n2, :n2], 0.0) for g in gs]
    a_rbs = [jnp.where(incl, g[n2:, :n2], 0.0) for g in gs]
    a_kks = [jnp.concatenate([jnp.where(strict, g[:n2, n2:], 0.0), jnp.where(incl, g[n2:, n2:], 0.0)], axis=0)
             for g in gs]
    m0 = sgn * lvl[0] == 1
    tinvs = [eye + jnp.where(m0, a, 0.0) for a in a_abs]
    for m in lvl[1:]:
        msk = sgn * m == 1
        cts = each(lambda a, t: _inv_dot(jnp.where(msk, a, 0.0), t), a_abs, tinvs)
        tinvs = each(lambda t, ct: t + _inv_dot(t, ct), tinvs, cts)
    kvs = each(_bdot, a_kks, v2s)
    wus = each(lambda t, at2, kv: _wu_dot(t, jnp.concatenate([at2, kv[:n2]], axis=1)), tinvs, at2s, kvs)
    rys = each(lambda rt2, kv, a_rb, wu: jnp.concatenate([rt2, kv[n2:]], axis=1) + _bdot(a_rb, wu),
               rt2s, kvs, a_rbs, wus)
    bws = each(lambda bh2, wu: _bdot(bh2, wu, TN_DIMS), bh2s, wus)
    khv = each(lambda kh2, v2: _bdot(kh2, v2, TN_DIMS), kh2s, v2s)
    out = []
    for ry, bw, kv, ctot in zip(rys, bws, khv, ctots):
        decay_col = jnp.sum(jnp.where(eye > 0, jnp.exp(ctot), 0.0), axis=1, keepdims=True)
        out.append((ry[:, :LANES], ry[:, LANES:], bw[:, :LANES], bw[:, LANES:] + kv, decay_col))
    return out


def _wkv_consts():
    n = WKV_CHUNK
    n2 = 2 * n
    tri = lax.broadcasted_iota(jnp.int32, (n, n), 0) - lax.broadcasted_iota(jnp.int32, (n, n), 1)
    lane = lax.broadcasted_iota(jnp.int32, (n2, LANES), 1)
    rr = lax.broadcasted_iota(jnp.int32, (n2, LANES), 0)
    keep = (lane >= RWKV_HEAD) == (rr >= n)
    ri = lax.broadcasted_iota(jnp.int32, (n2, n2), 0)
    ci = lax.broadcasted_iota(jnp.int32, (n2, n2), 1)
    r2 = ri & (n - 1)
    c2 = ci & (n - 1)
    dt = r2 - c2
    eye = jnp.where(ri == ci, 1.0, 0.0).astype(F32)
    lvl = []
    k = 0
    while (1 << k) < n:
        same = (r2 >> (k + 1)) == (c2 >> (k + 1))
        diff = ((r2 >> k) & 1) - ((c2 >> k) & 1)
        lvl.append(jnp.where(same, diff, 0))
        k += 1
    return tri, keep, dt, eye, lvl


def _wkv_kernel(r_ref, v_ref, kk_ref, lw_ref, kd_ref, alr_ref, s0_ref, y_ref, sfin_ref, st_ref, *, n_sub):
    d = pl.program_id(0)
    c = pl.program_id(3)
    nc = pl.num_programs(3)

    @pl.when(c == 0)
    def _():
        st_ref[...] = s0_ref[0, 0, 0]

    sgn = 1 - 2 * d
    rows = []
    for u in range(n_sub):
        ui = u + d * (n_sub - 1 - 2 * u)
        rows.append(pl.ds(pl.multiple_of(ui * WKV_CHUNK, WKV_CHUNK), WKV_CHUNK))
    chunks = [(r_ref[0, rw, :], lw_ref[0, 0, rw, :], kd_ref[0, 0, rw, :], v_ref[0, rw, :], kk_ref[0, rw, :],
               alr_ref[0, 0, rw, :]) for rw in rows]
    local = _wkv_local(chunks, sgn, _wkv_consts())
    st = st_ref[...]
    for rw, (rhat, yhat, mmat, nmat, decay_col) in zip(rows, local):
        y2 = _state_dot(rhat, st) + yhat
        y_ref[0, 0, rw, :] = y2[:WKV_CHUNK] + y2[WKV_CHUNK:]
        st = decay_col * st + _state_dot(mmat, st) + nmat
    st_ref[...] = st

    @pl.when(c == nc - 1)
    def _():
        sfin_ref[0, 0, 0] = st


def _wkv(r, v, kk, lw, kd, alr, s0):
    b, t, _ = r.shape
    npair = D_A // LANES
    n_sub = min(WKV_SUB, t // WKV_CHUNK)
    blk = n_sub * WKV_CHUNK
    nc = t // blk

    def tmap(d, bi, p, c):
        return (bi, c + d * (nc - 1 - 2 * c), p)

    def dmap(d, bi, p, c):
        return (d, bi, c + d * (nc - 1 - 2 * c), p)

    smap = lambda d, bi, p, c: (d, bi, p, 0, 0)
    shared = lambda: pl.BlockSpec((1, blk, LANES), tmap)
    direc = lambda: pl.BlockSpec((1, 1, blk, LANES), dmap)
    state = lambda: pl.BlockSpec((1, 1, 1, LANES, LANES), smap)
    return pl.pallas_call(
        functools.partial(_wkv_kernel, n_sub=n_sub),
        grid=(2, b, npair, nc),
        in_specs=[shared(), shared(), shared(), direc(), direc(), direc(), state()],
        out_specs=[direc(), state()],
        out_shape=[jax.ShapeDtypeStruct((2, b, t, D_A), F32),
                   jax.ShapeDtypeStruct((2, b, npair, LANES, LANES), F32)],
        scratch_shapes=[pltpu.VMEM((LANES, LANES), F32)],
        compiler_params=_params(("parallel", "parallel", "parallel", "arbitrary")),
        name="wkv",
    )(r, v, kk, lw, kd, alr, s0)


def _rwkv_finish_kernel(y_ref, r_ref, v_ref, ks_ref, g_ref, par_ref, o_ref):
    ones = _pair_ones()
    par = par_ref[...]
    inv_n = 1.0 / RWKV_HEAD
    for p in range(D_A // LANES):
        sl = slice(p * LANES, (p + 1) * LANES)
        y = y_ref[0, 0, :, sl] + y_ref[1, 0, :, sl]
        mu = _head_sum(y, ones) * inv_n
        yc = y - mu
        var = _head_sum(yc * yc, ones) * inv_n
        yn = yc * lax.rsqrt(var + LNX_EPS) * par[0:1, sl] + par[1:2, sl]
        bonus = _head_sum(r_ref[0, :, sl] * ks_ref[0, :, sl] * par[2:3, sl], ones) * v_ref[0, :, sl]
        o_ref[0, :, sl] = ((yn + bonus) * g_ref[0, :, sl]).astype(BF16)


def _rwkv_finish(y, r, v, ks, g, par, tm=256):
    b, t, d = r.shape
    tile = lambda: pl.BlockSpec((1, tm, d), lambda bi, i: (bi, i, 0))
    return pl.pallas_call(
        _rwkv_finish_kernel,
        grid=(b, t // tm),
        in_specs=[pl.BlockSpec((2, 1, tm, d), lambda bi, i: (0, bi, i, 0)), tile(), tile(), tile(), tile(),
                  pl.BlockSpec((8, d), lambda bi, i: (0, 0))],
        out_specs=tile(),
        out_shape=jax.ShapeDtypeStruct((b, t, d), BF16),
        compiler_params=_params(("parallel", "parallel")),
        name="rwkv_finish",
    )(y, r, v, ks, g, par)


def _head_rms(x, gain):
    return x * lax.rsqrt(jnp.mean(x * x, axis=-1, keepdims=True) + NORM_EPS) * gain


def _rope(x, cos_e, sin_s):
    lane = lax.broadcasted_iota(jnp.int32, x.shape, 1)
    swapped = jnp.where((lane & 1) == 0, pltpu.roll(x, LANES - 1, 1), pltpu.roll(x, 1, 1))
    return x * cos_e + swapped * sin_s


def _attn_prep_kernel(*refs, with_q):
    if with_q:
        pq_ref, pkv_ref, gains_ref, cos_ref, sin_ref, q_ref, k_ref, v_ref = refs
    else:
        pkv_ref, gains_ref, k_ref, v_ref = refs
    gains = gains_ref[...]
    kv_w = H_KV * HEAD_DIM
    for h in range(H_KV):
        sl = slice(h * HEAD_DIM, (h + 1) * HEAD_DIM)
        kh = _head_rms(pkv_ref[0, :, sl], gains[1:2])
        if with_q:
            kh = _rope(kh, cos_ref[...], sin_ref[...])
        k_ref[0, :, sl] = kh.astype(BF16)
    v_ref[0] = pkv_ref[0, :, kv_w:].astype(BF16)
    if with_q:
        scale = HEAD_DIM ** -0.5
        for h in range(H_Q):
            sl = slice(h * HEAD_DIM, (h + 1) * HEAD_DIM)
            qh = _rope(_head_rms(pq_ref[0, :, sl], gains[0:1]), cos_ref[...], sin_ref[...])
            q_ref[0, :, sl] = (qh * scale).astype(BF16)


def _attn_prep(pq, pkv, gains, cos_e, sin_s, tm=256):
    b, t, _ = pkv.shape
    with_q = pq is not None
    kv_w = H_KV * HEAD_DIM
    tm = min(tm, t)
    kv_spec = lambda: pl.BlockSpec((1, tm, kv_w), lambda bi, i: (bi, i, 0))
    in_specs = [pl.BlockSpec((1, tm, 2 * kv_w), lambda bi, i: (bi, i, 0)),
                pl.BlockSpec((8, HEAD_DIM), lambda bi, i: (0, 0))]
    args = [pkv, gains]
    out_specs = [kv_spec(), kv_spec()]
    out_shape = [jax.ShapeDtypeStruct((b, t, kv_w), BF16)] * 2
    if with_q:
        in_specs = [pl.BlockSpec((1, tm, D_B), lambda bi, i: (bi, i, 0))] + in_specs + [
            pl.BlockSpec((tm, HEAD_DIM), lambda bi, i: (i, 0)),
            pl.BlockSpec((tm, HEAD_DIM), lambda bi, i: (i, 0))]
        args = [pq] + args + [cos_e, sin_s]
        out_specs = [pl.BlockSpec((1, tm, D_B), lambda bi, i: (bi, i, 0))] + out_specs
        out_shape = [jax.ShapeDtypeStruct((b, t, D_B), BF16)] + out_shape
    return pl.pallas_call(
        functools.partial(_attn_prep_kernel, with_q=with_q),
        grid=(b, t // tm),
        in_specs=in_specs,
        out_specs=out_specs,
        out_shape=out_shape,
        compiler_params=_params(("parallel", "parallel")),
        name="attn_prep",
    )(*args)


def _attn_kernel(q_ref, k_ref, v_ref, o_ref):
    k = k_ref[0]
    v = v_ref[0]
    for g in range(Q_PER_KV):
        sl = slice(g * HEAD_DIM, (g + 1) * HEAD_DIM)
        s = lax.dot_general(q_ref[0, :, sl], k, NT_DIMS, preferred_element_type=F32)
        p = jnp.exp(s - jnp.max(s, axis=-1, keepdims=True))
        l = jnp.sum(p, axis=-1, keepdims=True)
        o = jnp.dot(p.astype(BF16), v, preferred_element_type=F32)
        o_ref[0, :, sl] = (o / l).astype(BF16)


def _attention(q, k_all, v_all, tq=256):
    b, t, _ = q.shape
    s_len = k_all.shape[1]
    gw = Q_PER_KV * HEAD_DIM
    return pl.pallas_call(
        _attn_kernel,
        grid=(b, H_KV, t // tq),
        in_specs=[pl.BlockSpec((1, tq, gw), lambda bi, h, i: (bi, i, h)),
                  pl.BlockSpec((1, s_len, HEAD_DIM), lambda bi, h, i: (bi, 0, h)),
                  pl.BlockSpec((1, s_len, HEAD_DIM), lambda bi, h, i: (bi, 0, h))],
        out_specs=pl.BlockSpec((1, tq, gw), lambda bi, h, i: (bi, i, h)),
        out_shape=jax.ShapeDtypeStruct((b, t, H_Q * HEAD_DIM), BF16),
        compiler_params=_params(("parallel", "parallel", "parallel")),
        name="attention",
    )(q, k_all, v_all)


def _dft_channels_kernel(h_ref, w_ref, c_ref, s_ref):
    p = jnp.dot(h_ref[0], w_ref[...], preferred_element_type=F32)
    c_ref[0] = p[:, :FOURIER_GROUP_DIM].astype(BF16)
    s_ref[0] = p[:, FOURIER_GROUP_DIM:].astype(BF16)


def _dft_channels(h, w_cs, tm=512):
    b, t, d = h.shape
    tm = min(tm, t)
    gd = FOURIER_GROUP_DIM
    blk = lambda: pl.BlockSpec((1, tm, gd), lambda bi, i, g: (bi, i, g))
    return pl.pallas_call(
        _dft_channels_kernel,
        grid=(b, t // tm, d // gd),
        in_specs=[blk(), pl.BlockSpec((gd, 2 * gd), lambda bi, i, g: (0, 0))],
        out_specs=[blk(), blk()],
        out_shape=[jax.ShapeDtypeStruct((b, t, d), BF16)] * 2,
        compiler_params=_params(("parallel", "parallel", "parallel")),
        name="dft_channels",
    )(h, w_cs)


def _dft_mats(n):
    i = lax.broadcasted_iota(jnp.int32, (n, n), 0)
    j = lax.broadcasted_iota(jnp.int32, (n, n), 1)
    ang = ((i * j) % n).astype(F32) * (2.0 * np.pi / n)
    scale = n ** -0.5
    return jnp.cos(ang) * scale, jnp.sin(ang) * scale


def _moe_kernel(h_ref, gates_ref, wg_ref, wu_ref, wd_ref, res_ref, gate_ref, o_ref, acc_ref):
    e = pl.program_id(2)
    ne = pl.num_programs(2)

    @pl.when(e == 0)
    def _():
        acc_ref[...] = jnp.zeros_like(acc_ref)

    h = h_ref[0]
    u = _silu(jnp.dot(h, wg_ref[0], preferred_element_type=F32)) * jnp.dot(h, wu_ref[0], preferred_element_type=F32)
    lane = lax.broadcasted_iota(jnp.int32, gates_ref.shape[1:], 1)
    gcol = jnp.sum(jnp.where(lane == e, gates_ref[0], 0.0), axis=1, keepdims=True)
    acc_ref[...] += gcol * jnp.dot(u.astype(BF16), wd_ref[0], preferred_element_type=F32)

    @pl.when(e == ne - 1)
    def _():
        o_ref[0] = res_ref[0] + gate_ref[0] * acc_ref[...]


def _moe(h, gates, wg, wu, wd, res, gate, tm=512):
    b, t, d = h.shape
    tm = min(tm, t)
    ne, _, de = wg.shape
    tok = lambda: pl.BlockSpec((1, tm, d), lambda bi, i, e: (bi, i, 0))
    return pl.pallas_call(
        _moe_kernel,
        grid=(b, t // tm, ne),
        in_specs=[tok(),
                  pl.BlockSpec((1, tm, ne), lambda bi, i, e: (bi, i, 0)),
                  pl.BlockSpec((1, d, de), lambda bi, i, e: (e, 0, 0)),
                  pl.BlockSpec((1, d, de), lambda bi, i, e: (e, 0, 0)),
                  pl.BlockSpec((1, de, d), lambda bi, i, e: (e, 0, 0)),
                  tok(),
                  pl.BlockSpec((1, 1, d), lambda bi, i, e: (bi, 0, 0))],
        out_specs=tok(),
        out_shape=jax.ShapeDtypeStruct((b, t, d), F32),
        scratch_shapes=[pltpu.VMEM((tm, d), F32)],
        compiler_params=_params(("parallel", "parallel", "arbitrary")),
        name="moe",
    )(h, gates, wg, wu, wd, res, gate)


def _rope_tables(n):
    rows = n // GRID_W
    row = jnp.repeat(jnp.arange(rows), GRID_W).astype(F32)
    col = jnp.tile(jnp.arange(GRID_W), rows).astype(F32)
    inv = ROPE_THETA ** (-jnp.arange(0, ROPE_AXIS_DIM, 2, dtype=F32) / ROPE_AXIS_DIM)
    ang = jnp.concatenate([row[:, None] * inv, col[:, None] * inv], axis=-1)
    cos_e = jnp.repeat(jnp.cos(ang), 2, axis=-1)
    sin_s = jnp.stack([-jnp.sin(ang), jnp.sin(ang)], axis=-1).reshape(n, HEAD_DIM)
    return cos_e, sin_s


def _pad_rows(rows, n):
    a = jnp.stack(rows, axis=0)
    return jnp.pad(a, ((0, n - a.shape[0]), (0, 0)))


def _even_mixer(x, ctx, mods, gain, w_in, mu_prev, mu_next, w0, w2, a0, a2, g2, k_k, k_a, r_k, ln_g, ln_b,
                q_gain, k_gain, w_out):
    bsz, seq, _ = x.shape
    sh1, sc1, gt1, csh1, csc1 = mods
    h_lat = _norm_mod(x, gain, sc1, sh1)
    h_ctx = _norm_mod(ctx, gain, csc1, csh1)

    a_cols = 3 * D_A + W_LORA + A_LORA + G_LORA
    wb = w_in.astype(BF16)
    w_r, w_k, w_v = wb[:, :D_A], wb[:, D_A:2 * D_A], wb[:, 2 * D_A:3 * D_A]
    w_lo = wb[:, 3 * D_A:a_cols]
    w_q = wb[:, a_cols:a_cols + D_B]
    w_kv = wb[:, a_cols + D_B:]

    def proj(h, w, name):
        rows = h.shape[1]
        return _matmul([h], [w], F32, batch=bsz, rows=rows, tm=1024, tn=512, name=name)

    par = _pad_rows([mu_prev[:D_A], mu_next[:D_A], mu_prev[D_A:2 * D_A], mu_next[D_A:2 * D_A],
                     mu_prev[2 * D_A:3 * D_A], mu_next[2 * D_A:3 * D_A],
                     w0[0], w0[1], a0[0], a0[1], k_k, k_a], 16)
    lpar = _pad_rows([mu_prev[3 * D_A:], mu_next[3 * D_A:]], 8)
    fin_par = _pad_rows([ln_g, ln_b, r_k.reshape(-1)], 8)
    w2 = jnp.pad(w2, ((0, 0), (0, A_LORA), (0, 0)))
    a2 = jnp.pad(a2, ((0, 0), (W_LORA, 0), (0, 0)))

    def streams(h):
        return _rwkv_prep(proj(h, w_r, "proj_r"), proj(h, w_k, "proj_k"), proj(h, w_v, "proj_v"),
                          proj(h, w_lo, "proj_lora"), par, lpar, w2, a2, g2)

    rc, vc, kkc, _, _, lwc, kdc, alrc = streams(h_ctx)
    rl, vl, kkl, gl, ksl, lwl, kdl, alrl = streams(h_lat)
    s0 = jnp.zeros((2, bsz, D_A // LANES, LANES, LANES), F32)
    _, s_ctx = _wkv(rc, vc, kkc, lwc, kdc, alrc, s0)
    y, _ = _wkv(rl, vl, kkl, lwl, kdl, alrl, s_ctx)
    ya = _rwkv_finish(y, rl, vl, ksl, gl, fin_par)

    cos_e, sin_s = _rope_tables(seq)
    gains = _pad_rows([q_gain, k_gain], 8)
    k_c, v_c = _attn_prep(None, proj(h_ctx, w_kv, "proj_kv"), gains, None, None)
    q_l, k_l, v_l = _attn_prep(proj(h_lat, w_q, "proj_q"), proj(h_lat, w_kv, "proj_kv"), gains, cos_e, sin_s)
    yb = _attention(q_l, jnp.concatenate([k_c, k_l], axis=1), jnp.concatenate([v_c, v_l], axis=1))

    wo = w_out.astype(BF16)
    return _matmul([ya, yb], [wo[:D_A], wo[D_A:]], F32, batch=bsz, rows=seq, tm=1024, tn=512,
                   res=x, gate=gt1, name="mixer_out")


def _fourier_mixer(x, mods, gain, w_out):
    bsz, seq, d = x.shape
    sh1, sc1, gt1 = mods
    h = _norm_mod(x, gain, sc1, sh1)
    cc, sc = _dft_mats(FOURIER_GROUP_DIM)
    ct, st = _dft_mats(seq)
    pc, ps = _dft_channels(h, jnp.concatenate([cc, sc], axis=1).astype(BF16))
    f = _matmul([ct.astype(BF16), (-st).astype(BF16)], [pc, ps], BF16, batch=bsz, rows=seq, tm=512, tn=512,
                name="dft_tokens")
    return _matmul([f], [w_out.astype(BF16)], F32, batch=bsz, rows=seq, tm=1024, tn=512, res=x, gate=gt1,
                   name="mixer_out")


def _moe_layer(x, mods, gain, router_w_t, router_b, w_gate, w_up, w_down):
    sh2, sc2, gt2 = mods
    tokens, gates_t = _norm_mod_router(x, gain, sc2, sh2, router_w_t, router_b)
    gates = jnp.swapaxes(gates_t, 1, 2)
    return _moe(tokens, gates, w_gate.astype(BF16), w_up.astype(BF16), w_down.astype(BF16), x, gt2)


def kernel(x, c, ctx, c_ctx, mod_w, mod_b, norm_mix, norm_ffn, ab_w_in, ab_mu_prev, ab_mu_next,
           rwkv_w0, rwkv_w2, rwkv_a0, rwkv_a2, rwkv_g2, rwkv_k_k, rwkv_k_a, rwkv_r_k, rwkv_ln_g, rwkv_ln_b,
           attn_q_norm, attn_k_norm, ab_w_out, fourier_w_out, router_w, router_b,
           exp_w_gate, exp_w_up, exp_w_down):
    bsz, seq, d = x.shape
    depth = mod_w.shape[0]
    cvec = jnp.concatenate([c, c_ctx[None], jnp.zeros((8 - bsz - 1, d), F32)], axis=0)
    mod = _modulation(cvec, mod_w, mod_b)
    router_w_t = router_w.T
    for l in range(depth):
        m_lat = mod[l, :bsz, None, :]
        sh1, sc1, gt1, sh2, sc2, gt2 = jnp.split(m_lat, 6, axis=-1)
        if l % 2 == 0:
            e = l // 2
            m_ctx = mod[l, bsz:bsz + 1, None, :]
            csh1, csc1 = m_ctx[..., :d], m_ctx[..., d:2 * d]
            x = _even_mixer(x, ctx, (sh1, sc1, gt1, csh1, csc1), norm_mix[l], ab_w_in[e], ab_mu_prev[e],
                            ab_mu_next[e], rwkv_w0[e], rwkv_w2[e], rwkv_a0[e], rwkv_a2[e], rwkv_g2[e],
                            rwkv_k_k[e], rwkv_k_a[e], rwkv_r_k[e], rwkv_ln_g[e], rwkv_ln_b[e],
                            attn_q_norm[e], attn_k_norm[e], ab_w_out[e])
        else:
            x = _fourier_mixer(x, (sh1, sc1, gt1), norm_mix[l], fourier_w_out[l // 2])
        x = _moe_layer(x, (sh2, sc2, gt2), norm_ffn[l], router_w_t, router_b,
                       exp_w_gate[l], exp_w_up[l], exp_w_down[l])
    return x
```

```python
import functools

import jax
import jax.numpy as jnp
import numpy as np
from jax import lax
from jax.experimental import pallas as pl
from jax.experimental.pallas import tpu as pltpu

F32 = jnp.float32
BF16 = jnp.bfloat16

D_MODEL = 2048
D_A = 1024
RWKV_HEAD = 64
W_LORA = 64
A_LORA = 64
G_LORA = 128
LNX_EPS = 64e-5
D_B = 1024
HEAD_DIM = 128
H_Q = 8
H_KV = 2
Q_PER_KV = H_Q // H_KV
GRID_W = 64
ROPE_THETA = 10000.0
ROPE_AXIS_DIM = HEAD_DIM // 2
FOURIER_GROUPS = 8
FOURIER_GROUP_DIM = D_MODEL // FOURIER_GROUPS
N_EXPERTS = 16
N_EXPERT_GROUPS = 4
EXPERTS_PER_GROUP = 4
D_EXPERT = 512
NORM_EPS = 1e-6
KK_EPS = 1e-12

LANES = 128
WKV_CHUNK = 64
WKV_SUB = 8
VMEM_LIMIT = 56 * 1024 * 1024

NT_DIMS = (((1,), (1,)), ((), ()))
TN_DIMS = (((0,), (0,)), ((), ()))


def _params(sem):
    return pltpu.CompilerParams(dimension_semantics=sem, vmem_limit_bytes=VMEM_LIMIT)


def _bdot(a, b, dims=None):
    a = a.astype(BF16)
    b = b.astype(BF16)
    if dims is None:
        return jnp.dot(a, b, preferred_element_type=F32)
    return lax.dot_general(a, b, dims, preferred_element_type=F32)


def _split2(x):
    hi = x.astype(BF16)
    lo = (x - hi.astype(F32)).astype(BF16)
    return hi, lo


def _split3(x):
    hi = x.astype(BF16)
    r1 = x - hi.astype(F32)
    mid = r1.astype(BF16)
    lo = (r1 - mid.astype(F32)).astype(BF16)
    return hi, mid, lo


def _dot3(a, b, dims=None):
    ah, al = _split2(a)
    bh, bl = _split2(b)
    return _bdot(ah, bh, dims) + (_bdot(ah, bl, dims) + _bdot(al, bh, dims))


def _dot_exact_lhs(a_bf16, x, dims=None):
    h, m, l = _split3(x)
    return _bdot(a_bf16, h, dims) + (_bdot(a_bf16, m, dims) + _bdot(a_bf16, l, dims))


def _dot_exact_rhs(x, b_bf16):
    h, m, l = _split3(x)
    return _bdot(h, b_bf16) + (_bdot(m, b_bf16) + _bdot(l, b_bf16))


def _sigmoid(x):
    return 1.0 / (1.0 + jnp.exp(-x))


def _silu(x):
    return x * _sigmoid(x)


def _mod_kernel(c_ref, w_ref, b_ref, o_ref):
    s = _silu(c_ref[...])
    o_ref[0] = _bdot(s, w_ref[0]) + b_ref[0]


def _modulation(cvec, mod_w, mod_b):
    depth, d, n = mod_w.shape
    tn = 1024
    return pl.pallas_call(
        _mod_kernel,
        grid=(depth, n // tn),
        in_specs=[pl.BlockSpec((8, d), lambda l, j: (0, 0)),
                  pl.BlockSpec((1, d, tn), lambda l, j: (l, 0, j)),
                  pl.BlockSpec((1, 1, tn), lambda l, j: (l, 0, j))],
        out_specs=pl.BlockSpec((1, 8, tn), lambda l, j: (l, 0, j)),
        out_shape=jax.ShapeDtypeStruct((depth, 8, n), F32),
        compiler_params=_params(("parallel", "parallel")),
        name="modulation",
    )(cvec, mod_w, mod_b.reshape(depth, 1, n))


def _normed(x, gain, sc, sh):
    y = x * lax.rsqrt(jnp.mean(x * x, axis=-1, keepdims=True) + NORM_EPS)
    return (y * gain) * (1.0 + sc) + sh


def _norm_mod_kernel(x_ref, g_ref, sc_ref, sh_ref, o_ref):
    o_ref[0] = _normed(x_ref[0], g_ref[...], sc_ref[0], sh_ref[0]).astype(BF16)


def _top2_gates(logits, bias):
    s = _sigmoid(logits)
    sel = s + bias
    neg = jnp.float32(-jnp.inf)
    flags, scores = [], []
    for g in range(N_EXPERT_GROUPS):
        xs = [sel[g * EXPERTS_PER_GROUP + i:g * EXPERTS_PER_GROUP + i + 1] for i in range(EXPERTS_PER_GROUP)]
        m1 = functools.reduce(jnp.maximum, xs)
        first, taken = [], None
        for x in xs:
            f = (x == m1) if taken is None else jnp.logical_and(x == m1, jnp.logical_not(taken))
            taken = f if taken is None else jnp.logical_or(taken, f)
            first.append(f)
        rest = [jnp.where(f, neg, x) for f, x in zip(first, xs)]
        m2 = functools.reduce(jnp.maximum, rest)
        second, taken = [], None
        for f, x in zip(first, xs):
            c = jnp.logical_and(x == m2, jnp.logical_not(f))
            if taken is not None:
                c = jnp.logical_and(c, jnp.logical_not(taken))
            taken = c if taken is None else jnp.logical_or(taken, c)
            second.append(c)
        flags.append([jnp.logical_or(f, c) for f, c in zip(first, second)])
        scores.append(m1 + m2)
    best = functools.reduce(jnp.maximum, scores)
    taken, rows = None, []
    for g in range(N_EXPERT_GROUPS):
        bg = (scores[g] == best) if taken is None else jnp.logical_and(scores[g] == best, jnp.logical_not(taken))
        taken = bg if taken is None else jnp.logical_or(taken, bg)
        for i in range(EXPERTS_PER_GROUP):
            e = g * EXPERTS_PER_GROUP + i
            rows.append(jnp.where(jnp.logical_and(bg, flags[g][i]), s[e:e + 1], 0.0))
    denom = functools.reduce(jnp.add, rows)
    inv = 1.0 / denom
    return [r * inv for r in rows]


ROW_TILES = D_MODEL // LANES


def _to_token_rows(ref, x, lead=()):
    n = x.shape[0]
    for s in range(ROW_TILES):
        ref[lead + (pl.ds(s, n, stride=ROW_TILES), slice(None))] = x[:, s * LANES:(s + 1) * LANES]


def _from_token_rows(ref, n, lead=()):
    return jnp.concatenate([ref[lead + (pl.ds(s, n, stride=ROW_TILES), slice(None))] for s in range(ROW_TILES)],
                           axis=-1)


def _norm_mod_router_kernel(x_ref, g_ref, sc_ref, sh_ref, rw_ref, rb_ref, o_ref, gates_ref):
    h = _normed(x_ref[0], g_ref[...], sc_ref[0], sh_ref[0])
    _to_token_rows(o_ref, h, lead=(0,))
    logits = _dot3(rw_ref[...], h, NT_DIMS)
    for e, row in enumerate(_top2_gates(logits, rb_ref[...])):
        gates_ref[0, e:e + 1, :] = row


def _norm_mod(x, gain, sc, sh, tm=512):
    b, t, d = x.shape
    tm = min(tm, t)
    per_b = sc.shape[0] > 1
    mod_map = (lambda bi, i: (bi, 0, 0)) if per_b else (lambda bi, i: (0, 0, 0))
    return pl.pallas_call(
        _norm_mod_kernel,
        grid=(b, t // tm),
        in_specs=[pl.BlockSpec((1, tm, d), lambda bi, i: (bi, i, 0)),
                  pl.BlockSpec((1, d), lambda bi, i: (0, 0)),
                  pl.BlockSpec((1, 1, d), mod_map),
                  pl.BlockSpec((1, 1, d), mod_map)],
        out_specs=pl.BlockSpec((1, tm, d), lambda bi, i: (bi, i, 0)),
        out_shape=jax.ShapeDtypeStruct((b, t, d), BF16),
        compiler_params=_params(("parallel", "parallel")),
        name="norm_mod",
    )(x, gain.reshape(1, d), sc, sh)


def _resid_kernel(x_ref, f_ref, gt_ref, xo_ref):
    xo_ref[0] = x_ref[0] + gt_ref[0] * _from_token_rows(f_ref, x_ref.shape[1])


def _resid_norm_mod_kernel(x_ref, f_ref, gt_ref, g_ref, sc_ref, sh_ref, xo_ref, o_ref):
    x = x_ref[0] + gt_ref[0] * _from_token_rows(f_ref, x_ref.shape[1])
    xo_ref[0] = x
    o_ref[0] = _normed(x, g_ref[...], sc_ref[0], sh_ref[0]).astype(BF16)


def _resid(x, f_rows, gt, norm=None, tm=512):
    b, t, d = x.shape
    tm = min(tm, t)
    nt = t // tm
    tok = lambda: pl.BlockSpec((1, tm, d), lambda bi, i: (bi, i, 0))
    per_b = lambda: pl.BlockSpec((1, 1, d), lambda bi, i: (bi, 0, 0))
    in_specs = [tok(), pl.BlockSpec((tm * ROW_TILES, LANES), lambda bi, i: (bi * nt + i, 0)), per_b()]
    args = [x, f_rows, gt]
    if norm is None:
        return pl.pallas_call(
            _resid_kernel, grid=(b, nt), in_specs=in_specs, out_specs=tok(),
            out_shape=jax.ShapeDtypeStruct((b, t, d), F32),
            compiler_params=_params(("parallel", "parallel")), name="resid",
        )(*args)
    gain, sc, sh = norm
    return pl.pallas_call(
        _resid_norm_mod_kernel, grid=(b, nt),
        in_specs=in_specs + [pl.BlockSpec((1, d), lambda bi, i: (0, 0)), per_b(), per_b()],
        out_specs=[tok(), tok()],
        out_shape=[jax.ShapeDtypeStruct((b, t, d), F32), jax.ShapeDtypeStruct((b, t, d), BF16)],
        compiler_params=_params(("parallel", "parallel")), name="resid_norm_mod",
    )(*args, gain.reshape(1, d), sc, sh)


def _norm_mod_router(x, gain, sc, sh, router_w_t, router_b, tm=512):
    b, t, d = x.shape
    tm = min(tm, t)
    e = router_w_t.shape[0]
    return pl.pallas_call(
        _norm_mod_router_kernel,
        grid=(b, t // tm),
        in_specs=[pl.BlockSpec((1, tm, d), lambda bi, i: (bi, i, 0)),
                  pl.BlockSpec((1, d), lambda bi, i: (0, 0)),
                  pl.BlockSpec((1, 1, d), lambda bi, i: (bi, 0, 0)),
                  pl.BlockSpec((1, 1, d), lambda bi, i: (bi, 0, 0)),
                  pl.BlockSpec((e, d), lambda bi, i: (0, 0)),
                  pl.BlockSpec((e, 1), lambda bi, i: (0, 0))],
        out_specs=[pl.BlockSpec((1, tm * ROW_TILES, LANES), lambda bi, i: (bi, i, 0)),
                   pl.BlockSpec((1, e, tm), lambda bi, i: (bi, 0, i))],
        out_shape=[jax.ShapeDtypeStruct((b, t * ROW_TILES, LANES), F32),
                   jax.ShapeDtypeStruct((b, e, t), F32)],
        compiler_params=_params(("parallel", "parallel")),
        name="norm_mod_router",
    )(x, gain.reshape(1, d), sc, sh, router_w_t, router_b.reshape(e, 1))


def _mm_kernel(*refs, n_terms, a_batched, w_batched, has_res):
    a_refs = refs[:n_terms]
    w_refs = refs[n_terms:2 * n_terms]
    o_ref = refs[-1]
    acc = None
    for a_ref, w_ref, ab, wb in zip(a_refs, w_refs, a_batched, w_batched):
        a = a_ref[0] if ab else a_ref[...]
        w = w_ref[0] if wb else w_ref[...]
        p = jnp.dot(a, w, preferred_element_type=F32)
        acc = p if acc is None else acc + p
    if has_res:
        res_ref, gate_ref = refs[2 * n_terms], refs[2 * n_terms + 1]
        acc = res_ref[0] + gate_ref[0] * acc
    o_ref[0] = acc.astype(o_ref.dtype)


def _matmul(a_list, w_list, out_dtype, *, batch, rows, tm, tn, res=None, gate=None, name="matmul"):
    n = w_list[0].shape[-1]
    tm = min(tm, rows)
    tn = min(tn, n)
    a_batched = tuple(a.ndim == 3 for a in a_list)
    w_batched = tuple(w.ndim == 3 for w in w_list)
    in_specs = []
    for a, ab in zip(a_list, a_batched):
        k = a.shape[-1]
        in_specs.append(pl.BlockSpec((1, tm, k), lambda b, i, j: (b, i, 0)) if ab
                        else pl.BlockSpec((tm, k), lambda b, i, j: (i, 0)))
    for w, wb in zip(w_list, w_batched):
        k = w.shape[-2]
        in_specs.append(pl.BlockSpec((1, k, tn), lambda b, i, j: (b, 0, j)) if wb
                        else pl.BlockSpec((k, tn), lambda b, i, j: (0, j)))
    args = list(a_list) + list(w_list)
    if res is not None:
        in_specs.append(pl.BlockSpec((1, tm, tn), lambda b, i, j: (b, i, j)))
        in_specs.append(pl.BlockSpec((1, 1, tn), lambda b, i, j: (b, 0, j)))
        args += [res, gate]
    kern = functools.partial(_mm_kernel, n_terms=len(a_list), a_batched=a_batched, w_batched=w_batched,
                             has_res=res is not None)
    return pl.pallas_call(
        kern,
        grid=(batch, rows // tm, n // tn),
        in_specs=in_specs,
        out_specs=pl.BlockSpec((1, tm, tn), lambda b, i, j: (b, i, j)),
        out_shape=jax.ShapeDtypeStruct((batch, rows, n), out_dtype),
        compiler_params=_params(("parallel", "parallel", "parallel")),
        name=name,
    )(*args)


def _pair_ones():
    r = lax.broadcasted_iota(jnp.int32, (LANES, LANES), 0) // RWKV_HEAD
    c = lax.broadcasted_iota(jnp.int32, (LANES, LANES), 1) // RWKV_HEAD
    return jnp.where(r == c, 1.0, 0.0).astype(BF16)


def _head_sum(x, ones_bf16):
    return _dot_exact_rhs(x, ones_bf16)


def _token_shift(ref, s, n, t, mu_p, mu_n):
    x = ref[0, s:s + n, :]
    w = x.shape[-1]
    ridx = lax.broadcasted_iota(jnp.int32, (n, w), 0)
    prev_row = ref[0, s - 1:s, :] if s > 0 else jnp.zeros((1, w), F32)
    next_row = ref[0, s + n:s + n + 1, :] if s + n < t else jnp.zeros((1, w), F32)
    prev = jnp.where(ridx == 0, prev_row, pltpu.roll(x, 1, 0))
    nxt = jnp.where(ridx == n - 1, next_row, pltpu.roll(x, n - 1, 0))
    return x + mu_p * (prev - x) + mu_n * (nxt - x)


def _rwkv_prep_kernel(pr_ref, pk_ref, pv_ref, pl_ref, par_ref, lpar_ref, w2_ref, a2_ref, g2_ref,
                      r_ref, v_ref, kk_ref, g_ref, ks_ref, lw_ref, kd_ref, alr_ref, *, t, rows):
    ones = _pair_ones()
    par = par_ref[...]
    lpar = lpar_ref[...]
    k_k, k_a = par[10:11], par[11:12]
    for s in range(0, t, rows):
        sl = slice(s, s + rows)
        r = _token_shift(pr_ref, s, rows, t, par[0:1], par[1:2])
        k = _token_shift(pk_ref, s, rows, t, par[2:3], par[3:4])
        v = _token_shift(pv_ref, s, rows, t, par[4:5], par[5:6])
        lo = _token_shift(pl_ref, s, rows, t, lpar[0:1], lpar[1:2])
        xwa, xg = lo[:, :LANES], lo[:, LANES:]
        r_ref[0, sl, :] = r
        v_ref[0, sl, :] = v
        g_ref[0, sl, :] = _dot3(_sigmoid(xg), g2_ref[...])
        kk = k * k_k
        kk = kk * lax.rsqrt(_head_sum(kk * kk, ones) + KK_EPS)
        kk_ref[0, sl, :] = kk
        tw = jnp.tanh(xwa)
        xa = xwa
        ksum = None
        for d in range(2):
            z = -(par[6 + d:7 + d] + _dot3(tw, w2_ref[d]))
            softplus = jnp.maximum(z, 0.0) + jnp.log(1.0 + jnp.exp(-jnp.abs(z)))
            w_log = -softplus - 0.5
            lw_ref[d, 0, sl, :] = -jnp.exp(w_log)
            a = _sigmoid(par[8 + d:9 + d] + _dot3(xa, a2_ref[d]))
            alr_ref[d, 0, sl, :] = a
            k_d = k * (1.0 + (a - 1.0) * k_a)
            kd_ref[d, 0, sl, :] = k_d
            ksum = k_d if ksum is None else ksum + k_d
        ks_ref[0, sl, :] = ksum


def _rwkv_prep(pr, pk, pv, plo, par, lpar, w2, a2, g2):
    b, t, _ = pr.shape
    npair = D_A // LANES
    rows = min(256, t)
    lora = plo.shape[-1]
    slab = lambda: pl.BlockSpec((1, t, LANES), lambda bi, p: (bi, 0, p))
    dslab = lambda: pl.BlockSpec((2, 1, t, LANES), lambda bi, p: (0, bi, 0, p))
    one = jax.ShapeDtypeStruct((b, t, D_A), F32)
    two = jax.ShapeDtypeStruct((2, b, t, D_A), F32)
    return pl.pallas_call(
        functools.partial(_rwkv_prep_kernel, t=t, rows=rows),
        grid=(b, npair),
        in_specs=[slab(), slab(), slab(),
                  pl.BlockSpec((1, t, lora), lambda bi, p: (bi, 0, 0)),
                  pl.BlockSpec((16, LANES), lambda bi, p: (0, p)),
                  pl.BlockSpec((8, lora), lambda bi, p: (0, 0)),
                  pl.BlockSpec((2, LANES, LANES), lambda bi, p: (0, 0, p)),
                  pl.BlockSpec((2, LANES, LANES), lambda bi, p: (0, 0, p)),
                  pl.BlockSpec((G_LORA, LANES), lambda bi, p: (0, p))],
        out_specs=[slab(), slab(), slab(), slab(), slab(), dslab(), dslab(), dslab()],
        out_shape=[one, one, one, one, one, two, two, two],
        compiler_params=_params(("parallel", "parallel")),
        name="rwkv_prep",
    )(pr, pk, pv, plo, par, lpar, w2, a2, g2)


_gram_dot = _bdot
_inv_dot = _bdot
_wu_dot = _bdot
_state_dot = _bdot


def _wkv_local(chunks, sgn, consts):
    n = WKV_CHUNK
    n2 = 2 * n
    tri, keep, dt, eye, lvl = consts
    order = sgn * dt
    strict = order > 0
    incl = order >= 0
    tri_b = jnp.where(sgn * tri >= 0, 1.0, 0.0).astype(BF16)
    each = lambda f, *ls: [f(*xs) for xs in zip(*ls)]

    def stack(x):
        return jnp.where(keep, jnp.concatenate([x, x], axis=0), 0.0)

    lws = [c[1] for c in chunks]
    cums = each(lambda lw: _dot_exact_lhs(tri_b, lw), lws)
    ctots = each(lambda lw: jnp.sum(lw, axis=0, keepdims=True), lws)
    lhs, rhs, kh2s, bh2s, v2s, at2s, rt2s = [], [], [], [], [], [], []
    for (r, lw, k, v, kk, alr), cum, ctot in zip(chunks, cums, ctots):
        e_neg = jnp.exp(-cum)
        e_rem = jnp.exp(ctot - cum)
        b = kk * alr
        rt2 = stack(r * jnp.exp(cum))
        at2 = stack(-kk * jnp.exp(cum - lw))
        lhs.append(jnp.concatenate([at2, rt2], axis=0))
        rhs.append(jnp.concatenate([stack(b * e_neg), stack(k * e_neg)], axis=0))
        kh2s.append(stack(k * e_rem))
        bh2s.append(stack(b * e_rem))
        v2s.append(stack(v))
        at2s.append(at2)
        rt2s.append(rt2)
    gs = each(lambda x, y: _gram_dot(x, y, NT_DIMS), lhs, rhs)
    a_abs = [jnp.where(strict, g[:n2, :n2], 0.0) for g in gs]
    a_rbs = [jnp.where(incl, g[n2:, :n2], 0.0) for g in gs]
    a_kks = [jnp.concatenate([jnp.where(strict, g[:n2, n2:], 0.0), jnp.where(incl, g[n2:, n2:], 0.0)], axis=0)
             for g in gs]
    m0 = sgn * lvl[0] == 1
    tinvs = [eye + jnp.where(m0, a, 0.0) for a in a_abs]
    for m in lvl[1:]:
        msk = sgn * m == 1
        cts = each(lambda a, t: _inv_dot(jnp.where(msk, a, 0.0), t), a_abs, tinvs)
        tinvs = each(lambda t, ct: t + _inv_dot(t, ct), tinvs, cts)
    kvs = each(_bdot, a_kks, v2s)
    wus = each(lambda t, at2, kv: _wu_dot(t, jnp.concatenate([at2, kv[:n2]], axis=1)), tinvs, at2s, kvs)
    rys = each(lambda rt2, kv, a_rb, wu: jnp.concatenate([rt2, kv[n2:]], axis=1) + _bdot(a_rb, wu),
               rt2s, kvs, a_rbs, wus)
    bws = each(lambda bh2, wu: _bdot(bh2, wu, TN_DIMS), bh2s, wus)
    khv = each(lambda kh2, v2: _bdot(kh2, v2, TN_DIMS), kh2s, v2s)
    out = []
    for ry, bw, kv, ctot in zip(rys, bws, khv, ctots):
        decay_col = jnp.sum(jnp.where(eye > 0, jnp.exp(ctot), 0.0), axis=1, keepdims=True)
        out.append((ry[:, :LANES], ry[:, LANES:], bw[:, :LANES], bw[:, LANES:] + kv, decay_col))
    return out


def _wkv_consts():
    n = WKV_CHUNK
    n2 = 2 * n
    tri = lax.broadcasted_iota(jnp.int32, (n, n), 0) - lax.broadcasted_iota(jnp.int32, (n, n), 1)
    lane = lax.broadcasted_iota(jnp.int32, (n2, LANES), 1)
    rr = lax.broadcasted_iota(jnp.int32, (n2, LANES), 0)
    keep = (lane >= RWKV_HEAD) == (rr >= n)
    ri = lax.broadcasted_iota(jnp.int32, (n2, n2), 0)
    ci = lax.broadcasted_iota(jnp.int32, (n2, n2), 1)
    r2 = ri & (n - 1)
    c2 = ci & (n - 1)
    dt = r2 - c2
    eye = jnp.where(ri == ci, 1.0, 0.0).astype(F32)
    lvl = []
    k = 0
    while (1 << k) < n:
        same = (r2 >> (k + 1)) == (c2 >> (k + 1))
        diff = ((r2 >> k) & 1) - ((c2 >> k) & 1)
        lvl.append(jnp.where(same, diff, 0))
        k += 1
    return tri, keep, dt, eye, lvl


def _wkv_kernel(r_ref, v_ref, kk_ref, lw_ref, kd_ref, alr_ref, s0_ref, y_ref, sfin_ref, st_ref, *, n_sub):
    d = pl.program_id(0)
    c = pl.program_id(3)
    nc = pl.num_programs(3)

    @pl.when(c == 0)
    def _():
        st_ref[...] = s0_ref[0, 0, 0]

    sgn = 1 - 2 * d
    rows = []
    for u in range(n_sub):
        ui = u + d * (n_sub - 1 - 2 * u)
        rows.append(pl.ds(pl.multiple_of(ui * WKV_CHUNK, WKV_CHUNK), WKV_CHUNK))
    chunks = [(r_ref[0, rw, :], lw_ref[0, 0, rw, :], kd_ref[0, 0, rw, :], v_ref[0, rw, :], kk_ref[0, rw, :],
               alr_ref[0, 0, rw, :]) for rw in rows]
    local = _wkv_local(chunks, sgn, _wkv_consts())
    st = st_ref[...]
    for rw, (rhat, yhat, mmat, nmat, decay_col) in zip(rows, local):
        y2 = _state_dot(rhat, st) + yhat
        y_ref[0, 0, rw, :] = y2[:WKV_CHUNK] + y2[WKV_CHUNK:]
        st = decay_col * st + _state_dot(mmat, st) + nmat
    st_ref[...] = st

    @pl.when(c == nc - 1)
    def _():
        sfin_ref[0, 0, 0] = st


def _wkv(r, v, kk, lw, kd, alr, s0):
    b, t, _ = r.shape
    npair = D_A // LANES
    n_sub = min(WKV_SUB, t // WKV_CHUNK)
    blk = n_sub * WKV_CHUNK
    nc = t // blk

    def tmap(d, bi, p, c):
        return (bi, c + d * (nc - 1 - 2 * c), p)

    def dmap(d, bi, p, c):
        return (d, bi, c + d * (nc - 1 - 2 * c), p)

    smap = lambda d, bi, p, c: (d, bi, p, 0, 0)
    shared = lambda: pl.BlockSpec((1, blk, LANES), tmap)
    direc = lambda: pl.BlockSpec((1, 1, blk, LANES), dmap)
    state = lambda: pl.BlockSpec((1, 1, 1, LANES, LANES), smap)
    return pl.pallas_call(
        functools.partial(_wkv_kernel, n_sub=n_sub),
        grid=(2, b, npair, nc),
        in_specs=[shared(), shared(), shared(), direc(), direc(), direc(), state()],
        out_specs=[direc(), state()],
        out_shape=[jax.ShapeDtypeStruct((2, b, t, D_A), F32),
                   jax.ShapeDtypeStruct((2, b, npair, LANES, LANES), F32)],
        scratch_shapes=[pltpu.VMEM((LANES, LANES), F32)],
        compiler_params=_params(("parallel", "parallel", "parallel", "arbitrary")),
        name="wkv",
    )(r, v, kk, lw, kd, alr, s0)


def _rwkv_finish_kernel(y_ref, r_ref, v_ref, ks_ref, g_ref, par_ref, o_ref):
    ones = _pair_ones()
    par = par_ref[...]
    inv_n = 1.0 / RWKV_HEAD
    for p in range(D_A // LANES):
        sl = slice(p * LANES, (p + 1) * LANES)
        y = y_ref[0, 0, :, sl] + y_ref[1, 0, :, sl]
        mu = _head_sum(y, ones) * inv_n
        yc = y - mu
        var = _head_sum(yc * yc, ones) * inv_n
        yn = yc * lax.rsqrt(var + LNX_EPS) * par[0:1, sl] + par[1:2, sl]
        bonus = _head_sum(r_ref[0, :, sl] * ks_ref[0, :, sl] * par[2:3, sl], ones) * v_ref[0, :, sl]
        o_ref[0, :, sl] = ((yn + bonus) * g_ref[0, :, sl]).astype(BF16)


def _rwkv_finish(y, r, v, ks, g, par, tm=256):
    b, t, d = r.shape
    tile = lambda: pl.BlockSpec((1, tm, d), lambda bi, i: (bi, i, 0))
    return pl.pallas_call(
        _rwkv_finish_kernel,
        grid=(b, t // tm),
        in_specs=[pl.BlockSpec((2, 1, tm, d), lambda bi, i: (0, bi, i, 0)), tile(), tile(), tile(), tile(),
                  pl.BlockSpec((8, d), lambda bi, i: (0, 0))],
        out_specs=tile(),
        out_shape=jax.ShapeDtypeStruct((b, t, d), BF16),
        compiler_params=_params(("parallel", "parallel")),
        name="rwkv_finish",
    )(y, r, v, ks, g, par)


def _head_rms(x, gain):
    return x * lax.rsqrt(jnp.mean(x * x, axis=-1, keepdims=True) + NORM_EPS) * gain


def _rope(x, cos_e, sin_s):
    lane = lax.broadcasted_iota(jnp.int32, x.shape, 1)
    swapped = jnp.where((lane & 1) == 0, pltpu.roll(x, LANES - 1, 1), pltpu.roll(x, 1, 1))
    return x * cos_e + swapped * sin_s


def _attn_prep_kernel(*refs, with_q):
    if with_q:
        pq_ref, pkv_ref, gains_ref, cos_ref, sin_ref, q_ref, k_ref, v_ref = refs
    else:
        pkv_ref, gains_ref, k_ref, v_ref = refs
    gains = gains_ref[...]
    kv_w = H_KV * HEAD_DIM
    for h in range(H_KV):
        sl = slice(h * HEAD_DIM, (h + 1) * HEAD_DIM)
        kh = _head_rms(pkv_ref[0, :, sl], gains[1:2])
        if with_q:
            kh = _rope(kh, cos_ref[...], sin_ref[...])
        k_ref[0, :, sl] = kh.astype(BF16)
    v_ref[0] = pkv_ref[0, :, kv_w:].astype(BF16)
    if with_q:
        scale = HEAD_DIM ** -0.5
        for h in range(H_Q):
            sl = slice(h * HEAD_DIM, (h + 1) * HEAD_DIM)
            qh = _rope(_head_rms(pq_ref[0, :, sl], gains[0:1]), cos_ref[...], sin_ref[...])
            q_ref[0, :, sl] = (qh * scale).astype(BF16)


def _attn_prep(pq, pkv, gains, cos_e, sin_s, tm=256):
    b, t, _ = pkv.shape
    with_q = pq is not None
    kv_w = H_KV * HEAD_DIM
    tm = min(tm, t)
    kv_spec = lambda: pl.BlockSpec((1, tm, kv_w), lambda bi, i: (bi, i, 0))
    in_specs = [pl.BlockSpec((1, tm, 2 * kv_w), lambda bi, i: (bi, i, 0)),
                pl.BlockSpec((8, HEAD_DIM), lambda bi, i: (0, 0))]
    args = [pkv, gains]
    out_specs = [kv_spec(), kv_spec()]
    out_shape = [jax.ShapeDtypeStruct((b, t, kv_w), BF16)] * 2
    if with_q:
        in_specs = [pl.BlockSpec((1, tm, D_B), lambda bi, i: (bi, i, 0))] + in_specs + [
            pl.BlockSpec((tm, HEAD_DIM), lambda bi, i: (i, 0)),
            pl.BlockSpec((tm, HEAD_DIM), lambda bi, i: (i, 0))]
        args = [pq] + args + [cos_e, sin_s]
        out_specs = [pl.BlockSpec((1, tm, D_B), lambda bi, i: (bi, i, 0))] + out_specs
        out_shape = [jax.ShapeDtypeStruct((b, t, D_B), BF16)] + out_shape
    return pl.pallas_call(
        functools.partial(_attn_prep_kernel, with_q=with_q),
        grid=(b, t // tm),
        in_specs=in_specs,
        out_specs=out_specs,
        out_shape=out_shape,
        compiler_params=_params(("parallel", "parallel")),
        name="attn_prep",
    )(*args)


def _attn_kernel(q_ref, k_ref, v_ref, o_ref):
    k = k_ref[0]
    v = v_ref[0]
    for g in range(Q_PER_KV):
        sl = slice(g * HEAD_DIM, (g + 1) * HEAD_DIM)
        s = lax.dot_general(q_ref[0, :, sl], k, NT_DIMS, preferred_element_type=F32)
        p = jnp.exp(s - jnp.max(s, axis=-1, keepdims=True))
        l = jnp.sum(p, axis=-1, keepdims=True)
        o = jnp.dot(p.astype(BF16), v, preferred_element_type=F32)
        o_ref[0, :, sl] = (o / l).astype(BF16)


def _attention(q, k_all, v_all, tq=256):
    b, t, _ = q.shape
    s_len = k_all.shape[1]
    gw = Q_PER_KV * HEAD_DIM
    return pl.pallas_call(
        _attn_kernel,
        grid=(b, H_KV, t // tq),
        in_specs=[pl.BlockSpec((1, tq, gw), lambda bi, h, i: (bi, i, h)),
                  pl.BlockSpec((1, s_len, HEAD_DIM), lambda bi, h, i: (bi, 0, h)),
                  pl.BlockSpec((1, s_len, HEAD_DIM), lambda bi, h, i: (bi, 0, h))],
        out_specs=pl.BlockSpec((1, tq, gw), lambda bi, h, i: (bi, i, h)),
        out_shape=jax.ShapeDtypeStruct((b, t, H_Q * HEAD_DIM), BF16),
        compiler_params=_params(("parallel", "parallel", "parallel")),
        name="attention",
    )(q, k_all, v_all)


def _dft_channels_kernel(h_ref, w_ref, c_ref, s_ref):
    p = jnp.dot(h_ref[0], w_ref[...], preferred_element_type=F32)
    c_ref[0] = p[:, :FOURIER_GROUP_DIM].astype(BF16)
    s_ref[0] = p[:, FOURIER_GROUP_DIM:].astype(BF16)


def _dft_channels(h, w_cs, tm=512):
    b, t, d = h.shape
    tm = min(tm, t)
    gd = FOURIER_GROUP_DIM
    blk = lambda: pl.BlockSpec((1, tm, gd), lambda bi, i, g: (bi, i, g))
    return pl.pallas_call(
        _dft_channels_kernel,
        grid=(b, t // tm, d // gd),
        in_specs=[blk(), pl.BlockSpec((gd, 2 * gd), lambda bi, i, g: (0, 0))],
        out_specs=[blk(), blk()],
        out_shape=[jax.ShapeDtypeStruct((b, t, d), BF16)] * 2,
        compiler_params=_params(("parallel", "parallel", "parallel")),
        name="dft_channels",
    )(h, w_cs)


def _dft_mats(n):
    i = lax.broadcasted_iota(jnp.int32, (n, n), 0)
    j = lax.broadcasted_iota(jnp.int32, (n, n), 1)
    ang = ((i * j) % n).astype(F32) * (2.0 * np.pi / n)
    scale = n ** -0.5
    return jnp.cos(ang) * scale, jnp.sin(ang) * scale


MOE_TILE = 512


def _moe_kernel(src_ref, grp_ref, live_ref, h_hbm, gs_ref, wg_ref, wu_ref, wd_ref, o_hbm,
                rows_ref, h16_ref, acc_ref, sem, *, tm):
    i = pl.program_id(0)
    j = pl.program_id(1)
    live = i < live_ref[0]

    def row(ref, t):
        return ref.at[pl.ds(pl.multiple_of(t * ROW_TILES, ROW_TILES), ROW_TILES)]

    def gather_copy(k, tok):
        return pltpu.make_async_copy(row(h_hbm, tok), row(rows_ref, k), sem.at[0])

    def scatter_copy(k, tok):
        return pltpu.make_async_copy(row(rows_ref, k), row(o_hbm, tok), sem.at[1])

    @pl.when(jnp.logical_and(live, j == 0))
    def _():
        def issue(k, c):
            gather_copy(k, jnp.maximum(src_ref[i * tm + k], 0)).start()
            return c
        lax.fori_loop(0, tm, issue, 0)

        def drain(k, c):
            gather_copy(k, 0).wait()
            return c
        lax.fori_loop(0, tm, drain, 0)
        h16_ref[...] = _from_token_rows(rows_ref, tm).astype(BF16)
        acc_ref[...] = jnp.zeros_like(acc_ref)

    @pl.when(live)
    def _():
        h = h16_ref[...]
        u = _silu(_bdot(h, wg_ref[0])) * _bdot(h, wu_ref[0])
        lane = lax.broadcasted_iota(jnp.int32, gs_ref.shape, 1)
        gcol = jnp.sum(jnp.where(lane == j, gs_ref[...], 0.0), axis=1, keepdims=True)
        acc_ref[...] += gcol * _bdot(u, wd_ref[0])

    @pl.when(jnp.logical_and(live, j == pl.num_programs(1) - 1))
    def _():
        _to_token_rows(rows_ref, acc_ref[...])

        def issue(k, c):
            tok = src_ref[i * tm + k]

            @pl.when(tok >= 0)
            def _():
                scatter_copy(k, tok).start()
            return c
        lax.fori_loop(0, tm, issue, 0)

        def drain(k, c):
            @pl.when(src_ref[i * tm + k] >= 0)
            def _():
                scatter_copy(k, 0).wait()
            return c
        lax.fori_loop(0, tm, drain, 0)


def _route_meta(gates, tm):
    n = gates.shape[0]
    ng, epg = N_EXPERT_GROUPS, EXPERTS_PER_GROUP
    gg = gates.reshape(n, ng, epg)
    grp = jnp.argmax(jnp.max(gg, axis=-1) > 0, axis=-1).astype(jnp.int32)
    onehot = (grp[:, None] == jnp.arange(ng)[None, :]).astype(jnp.int32)
    counts = jnp.sum(onehot, axis=0)
    padded = ((counts + tm - 1) // tm) * tm
    ends = jnp.cumsum(padded)
    starts = ends - padded
    rank = jnp.sum((jnp.cumsum(onehot, axis=0) - onehot) * onehot, axis=1)
    pos = jnp.sum(starts[None, :] * onehot, axis=1) + rank
    p_total = n + ng * tm
    src = jnp.full((p_total,), -1, jnp.int32).at[pos].set(jnp.arange(n, dtype=jnp.int32))
    gsel = jnp.sum(gg * onehot[:, :, None].astype(F32), axis=1)
    gs_sorted = jnp.zeros((p_total, epg), F32).at[pos].set(gsel)
    n_live = ends[-1] // tm
    tile_start = jnp.minimum(jnp.arange(p_total // tm, dtype=jnp.int32), n_live - 1) * tm
    tile_grp = jnp.minimum(jnp.sum(tile_start[:, None] >= ends[None, :], axis=1), ng - 1).astype(jnp.int32)
    return src, gs_sorted, tile_grp, n_live.astype(jnp.int32).reshape(1)


def _moe(h_rows, gates, wg, wu, wd, tm=MOE_TILE):
    n = gates.shape[0]
    d = D_MODEL
    ne, _, de = wg.shape
    epg = EXPERTS_PER_GROUP
    src, gs_sorted, tile_grp, n_live = _route_meta(gates, tm)
    n_tiles = src.shape[0] // tm

    def wmap(i, j, src, grp, live):
        return (grp[i] * epg + jnp.where(i < live[0], j, epg - 1), 0, 0)

    grid_spec = pltpu.PrefetchScalarGridSpec(
        num_scalar_prefetch=3,
        grid=(n_tiles, epg),
        in_specs=[pl.BlockSpec(memory_space=pl.ANY),
                  pl.BlockSpec((tm, epg), lambda i, j, src, grp, live: (i, 0)),
                  pl.BlockSpec((1, d, de), wmap),
                  pl.BlockSpec((1, d, de), wmap),
                  pl.BlockSpec((1, de, d), wmap)],
        out_specs=pl.BlockSpec(memory_space=pl.ANY),
        scratch_shapes=[pltpu.VMEM((tm * ROW_TILES, LANES), F32), pltpu.VMEM((tm, d), BF16),
                        pltpu.VMEM((tm, d), F32), pltpu.SemaphoreType.DMA((2,))],
    )
    return pl.pallas_call(
        functools.partial(_moe_kernel, tm=tm),
        grid_spec=grid_spec,
        out_shape=jax.ShapeDtypeStruct((n * ROW_TILES, LANES), F32),
        compiler_params=pltpu.CompilerParams(dimension_semantics=("arbitrary", "arbitrary"),
                                             vmem_limit_bytes=VMEM_LIMIT, disable_bounds_checks=True),
        name="moe",
    )(src, tile_grp, n_live, h_rows, gs_sorted, wg, wu, wd)


def _rope_tables(n):
    rows = n // GRID_W
    row = jnp.repeat(jnp.arange(rows), GRID_W).astype(F32)
    col = jnp.tile(jnp.arange(GRID_W), rows).astype(F32)
    inv = ROPE_THETA ** (-jnp.arange(0, ROPE_AXIS_DIM, 2, dtype=F32) / ROPE_AXIS_DIM)
    ang = jnp.concatenate([row[:, None] * inv, col[:, None] * inv], axis=-1)
    cos_e = jnp.repeat(jnp.cos(ang), 2, axis=-1)
    sin_s = jnp.stack([-jnp.sin(ang), jnp.sin(ang)], axis=-1).reshape(n, HEAD_DIM)
    return cos_e, sin_s


def _pad_rows(rows, n):
    a = jnp.stack(rows, axis=0)
    return jnp.pad(a, ((0, n - a.shape[0]), (0, 0)))


def _enter_layer(x, pending, gain, sc, sh):
    if pending is None:
        return x, _norm_mod(x, gain, sc, sh)
    f_rows, gt = pending
    return _resid(x, f_rows, gt, norm=(gain, sc, sh))


def _even_mixer(x, pending, ctx, mods, gain, w_in, mu_prev, mu_next, w0, w2, a0, a2, g2, k_k, k_a, r_k, ln_g, ln_b,
                q_gain, k_gain, w_out):
    bsz, seq, _ = x.shape
    sh1, sc1, gt1, csh1, csc1 = mods
    x, h_lat = _enter_layer(x, pending, gain, sc1, sh1)
    h_ctx = _norm_mod(ctx, gain, csc1, csh1)

    a_cols = 3 * D_A + W_LORA + A_LORA + G_LORA
    wb = w_in.astype(BF16)
    w_r, w_k, w_v = wb[:, :D_A], wb[:, D_A:2 * D_A], wb[:, 2 * D_A:3 * D_A]
    w_lo = wb[:, 3 * D_A:a_cols]
    w_q = wb[:, a_cols:a_cols + D_B]
    w_kv = wb[:, a_cols + D_B:]

    def proj(h, w, name):
        rows = h.shape[1]
        return _matmul([h], [w], F32, batch=bsz, rows=rows, tm=1024, tn=512, name=name)

    par = _pad_rows([mu_prev[:D_A], mu_next[:D_A], mu_prev[D_A:2 * D_A], mu_next[D_A:2 * D_A],
                     mu_prev[2 * D_A:3 * D_A], mu_next[2 * D_A:3 * D_A],
                     w0[0], w0[1], a0[0], a0[1], k_k, k_a], 16)
    lpar = _pad_rows([mu_prev[3 * D_A:], mu_next[3 * D_A:]], 8)
    fin_par = _pad_rows([ln_g, ln_b, r_k.reshape(-1)], 8)
    w2 = jnp.pad(w2, ((0, 0), (0, A_LORA), (0, 0)))
    a2 = jnp.pad(a2, ((0, 0), (W_LORA, 0), (0, 0)))

    def streams(h):
        return _rwkv_prep(proj(h, w_r, "proj_r"), proj(h, w_k, "proj_k"), proj(h, w_v, "proj_v"),
                          proj(h, w_lo, "proj_lora"), par, lpar, w2, a2, g2)

    rc, vc, kkc, _, _, lwc, kdc, alrc = streams(h_ctx)
    rl, vl, kkl, gl, ksl, lwl, kdl, alrl = streams(h_lat)
    s0 = jnp.zeros((2, bsz, D_A // LANES, LANES, LANES), F32)
    _, s_ctx = _wkv(rc, vc, kkc, lwc, kdc, alrc, s0)
    y, _ = _wkv(rl, vl, kkl, lwl, kdl, alrl, s_ctx)
    ya = _rwkv_finish(y, rl, vl, ksl, gl, fin_par)

    cos_e, sin_s = _rope_tables(seq)
    gains = _pad_rows([q_gain, k_gain], 8)
    k_c, v_c = _attn_prep(None, proj(h_ctx, w_kv, "proj_kv"), gains, None, None)
    q_l, k_l, v_l = _attn_prep(proj(h_lat, w_q, "proj_q"), proj(h_lat, w_kv, "proj_kv"), gains, cos_e, sin_s)
    yb = _attention(q_l, jnp.concatenate([k_c, k_l], axis=1), jnp.concatenate([v_c, v_l], axis=1))

    wo = w_out.astype(BF16)
    return _matmul([ya, yb], [wo[:D_A], wo[D_A:]], F32, batch=bsz, rows=seq, tm=1024, tn=512,
                   res=x, gate=gt1, name="mixer_out")


def _fourier_mixer(x, pending, mods, gain, w_out):
    bsz, seq, d = x.shape
    sh1, sc1, gt1 = mods
    x, h = _enter_layer(x, pending, gain, sc1, sh1)
    cc, sc = _dft_mats(FOURIER_GROUP_DIM)
    ct, st = _dft_mats(seq)
    pc, ps = _dft_channels(h, jnp.concatenate([cc, sc], axis=1).astype(BF16))
    f = _matmul([ct.astype(BF16), (-st).astype(BF16)], [pc, ps], BF16, batch=bsz, rows=seq, tm=512, tn=512,
                name="dft_tokens")
    return _matmul([f], [w_out.astype(BF16)], F32, batch=bsz, rows=seq, tm=1024, tn=512, res=x, gate=gt1,
                   name="mixer_out")


def _moe_layer(x, mods, gain, router_w_t, router_b, w_gate, w_up, w_down):
    bsz, seq, d = x.shape
    sh2, sc2, _ = mods
    h_rows, gates_t = _norm_mod_router(x, gain, sc2, sh2, router_w_t, router_b)
    gates = jnp.swapaxes(gates_t, 1, 2).reshape(bsz * seq, -1)
    return _moe(h_rows.reshape(bsz * seq * ROW_TILES, LANES), gates, w_gate, w_up, w_down)


def kernel(x, c, ctx, c_ctx, mod_w, mod_b, norm_mix, norm_ffn, ab_w_in, ab_mu_prev, ab_mu_next,
           rwkv_w0, rwkv_w2, rwkv_a0, rwkv_a2, rwkv_g2, rwkv_k_k, rwkv_k_a, rwkv_r_k, rwkv_ln_g, rwkv_ln_b,
           attn_q_norm, attn_k_norm, ab_w_out, fourier_w_out, router_w, router_b,
           exp_w_gate, exp_w_up, exp_w_down):
    bsz, seq, d = x.shape
    depth = mod_w.shape[0]
    cvec = jnp.concatenate([c, c_ctx[None], jnp.zeros((8 - bsz - 1, d), F32)], axis=0)
    mod = _modulation(cvec, mod_w, mod_b)
    router_w_t = router_w.T
    pending = None
    for l in range(depth):
        m_lat = mod[l, :bsz, None, :]
        sh1, sc1, gt1, sh2, sc2, gt2 = jnp.split(m_lat, 6, axis=-1)
        if l % 2 == 0:
            e = l // 2
            m_ctx = mod[l, bsz:bsz + 1, None, :]
            csh1, csc1 = m_ctx[..., :d], m_ctx[..., d:2 * d]
            x = _even_mixer(x, pending, ctx, (sh1, sc1, gt1, csh1, csc1), norm_mix[l], ab_w_in[e], ab_mu_prev[e],
                            ab_mu_next[e], rwkv_w0[e], rwkv_w2[e], rwkv_a0[e], rwkv_a2[e], rwkv_g2[e],
                            rwkv_k_k[e], rwkv_k_a[e], rwkv_r_k[e], rwkv_ln_g[e], rwkv_ln_b[e],
                            attn_q_norm[e], attn_k_norm[e], ab_w_out[e])
        else:
            x = _fourier_mixer(x, pending, (sh1, sc1, gt1), norm_mix[l], fourier_w_out[l // 2])
        f_rows = _moe_layer(x, (sh2, sc2, gt2), norm_ffn[l], router_w_t, router_b,
                            exp_w_gate[l], exp_w_up[l], exp_w_down[l])
        pending = (f_rows, gt2)
    return _resid(x, pending[0], pending[1])
```

```python
import functools

import jax
import jax.numpy as jnp
import numpy as np
from jax import lax
from jax.experimental import pallas as pl
from jax.experimental.pallas import tpu as pltpu

F32 = jnp.float32
BF16 = jnp.bfloat16

D_MODEL = 2048
D_A = 1024
RWKV_HEAD = 64
W_LORA = 64
A_LORA = 64
G_LORA = 128
LNX_EPS = 64e-5
D_B = 1024
HEAD_DIM = 128
H_Q = 8
H_KV = 2
Q_PER_KV = H_Q // H_KV
GRID_W = 64
ROPE_THETA = 10000.0
ROPE_AXIS_DIM = HEAD_DIM // 2
FOURIER_GROUPS = 8
FOURIER_GROUP_DIM = D_MODEL // FOURIER_GROUPS
N_EXPERTS = 16
N_EXPERT_GROUPS = 4
EXPERTS_PER_GROUP = 4
D_EXPERT = 512
NORM_EPS = 1e-6
KK_EPS = 1e-12

LANES = 128
WKV_CHUNK = 64
WKV_SUB = 8
VMEM_LIMIT = 56 * 1024 * 1024

NT_DIMS = (((1,), (1,)), ((), ()))
TN_DIMS = (((0,), (0,)), ((), ()))


def _params(sem):
    return pltpu.CompilerParams(dimension_semantics=sem, vmem_limit_bytes=VMEM_LIMIT)


def _bdot(a, b, dims=None):
    a = a.astype(BF16)
    b = b.astype(BF16)
    if dims is None:
        return jnp.dot(a, b, preferred_element_type=F32)
    return lax.dot_general(a, b, dims, preferred_element_type=F32)


def _split2(x):
    hi = x.astype(BF16)
    lo = (x - hi.astype(F32)).astype(BF16)
    return hi, lo


def _split3(x):
    hi = x.astype(BF16)
    r1 = x - hi.astype(F32)
    mid = r1.astype(BF16)
    lo = (r1 - mid.astype(F32)).astype(BF16)
    return hi, mid, lo


def _dot3(a, b, dims=None):
    ah, al = _split2(a)
    bh, bl = _split2(b)
    return _bdot(ah, bh, dims) + (_bdot(ah, bl, dims) + _bdot(al, bh, dims))


def _dot_exact_lhs(a_bf16, x, dims=None):
    h, m, l = _split3(x)
    return _bdot(a_bf16, h, dims) + (_bdot(a_bf16, m, dims) + _bdot(a_bf16, l, dims))


def _dot_exact_rhs(x, b_bf16):
    h, m, l = _split3(x)
    return _bdot(h, b_bf16) + (_bdot(m, b_bf16) + _bdot(l, b_bf16))


def _sigmoid(x):
    return 1.0 / (1.0 + jnp.exp(-x))


def _silu(x):
    return x * _sigmoid(x)


def _mod_kernel(c_ref, w_ref, b_ref, o_ref):
    s = _silu(c_ref[...])
    o_ref[0] = _bdot(s, w_ref[0]) + b_ref[0]


def _modulation(cvec, mod_w, mod_b):
    depth, d, n = mod_w.shape
    tn = 1024
    return pl.pallas_call(
        _mod_kernel,
        grid=(depth, n // tn),
        in_specs=[pl.BlockSpec((8, d), lambda l, j: (0, 0)),
                  pl.BlockSpec((1, d, tn), lambda l, j: (l, 0, j)),
                  pl.BlockSpec((1, 1, tn), lambda l, j: (l, 0, j))],
        out_specs=pl.BlockSpec((1, 8, tn), lambda l, j: (l, 0, j)),
        out_shape=jax.ShapeDtypeStruct((depth, 8, n), F32),
        compiler_params=_params(("parallel", "parallel")),
        name="modulation",
    )(cvec, mod_w, mod_b.reshape(depth, 1, n))


def _normed(x, gain, sc, sh):
    y = x * lax.rsqrt(jnp.mean(x * x, axis=-1, keepdims=True) + NORM_EPS)
    return (y * gain) * (1.0 + sc) + sh


def _norm_mod_kernel(x_ref, g_ref, sc_ref, sh_ref, o_ref):
    o_ref[0] = _normed(x_ref[0], g_ref[...], sc_ref[0], sh_ref[0]).astype(BF16)


def _top2_gates(logits, bias):
    s = _sigmoid(logits)
    sel = s + bias
    neg = jnp.float32(-jnp.inf)
    flags, scores = [], []
    for g in range(N_EXPERT_GROUPS):
        xs = [sel[g * EXPERTS_PER_GROUP + i:g * EXPERTS_PER_GROUP + i + 1] for i in range(EXPERTS_PER_GROUP)]
        m1 = functools.reduce(jnp.maximum, xs)
        first, taken = [], None
        for x in xs:
            f = (x == m1) if taken is None else jnp.logical_and(x == m1, jnp.logical_not(taken))
            taken = f if taken is None else jnp.logical_or(taken, f)
            first.append(f)
        rest = [jnp.where(f, neg, x) for f, x in zip(first, xs)]
        m2 = functools.reduce(jnp.maximum, rest)
        second, taken = [], None
        for f, x in zip(first, xs):
            c = jnp.logical_and(x == m2, jnp.logical_not(f))
            if taken is not None:
                c = jnp.logical_and(c, jnp.logical_not(taken))
            taken = c if taken is None else jnp.logical_or(taken, c)
            second.append(c)
        flags.append([jnp.logical_or(f, c) for f, c in zip(first, second)])
        scores.append(m1 + m2)
    best = functools.reduce(jnp.maximum, scores)
    taken, rows = None, []
    for g in range(N_EXPERT_GROUPS):
        bg = (scores[g] == best) if taken is None else jnp.logical_and(scores[g] == best, jnp.logical_not(taken))
        taken = bg if taken is None else jnp.logical_or(taken, bg)
        for i in range(EXPERTS_PER_GROUP):
            e = g * EXPERTS_PER_GROUP + i
            rows.append(jnp.where(jnp.logical_and(bg, flags[g][i]), s[e:e + 1], 0.0))
    denom = functools.reduce(jnp.add, rows)
    inv = 1.0 / denom
    return [r * inv for r in rows]


ROW_TILES = D_MODEL // LANES


def _to_token_rows(ref, x, lead=()):
    n = x.shape[0]
    for s in range(ROW_TILES):
        ref[lead + (pl.ds(s, n, stride=ROW_TILES), slice(None))] = x[:, s * LANES:(s + 1) * LANES]


def _from_token_rows(ref, n, lead=()):
    return jnp.concatenate([ref[lead + (pl.ds(s, n, stride=ROW_TILES), slice(None))] for s in range(ROW_TILES)],
                           axis=-1)


def _norm_mod_router_kernel(x_ref, g_ref, sc_ref, sh_ref, rw_ref, rb_ref, o_ref, gates_ref):
    h = _normed(x_ref[0], g_ref[...], sc_ref[0], sh_ref[0])
    _to_token_rows(o_ref, h, lead=(0,))
    logits = _dot3(rw_ref[...], h, NT_DIMS)
    for e, row in enumerate(_top2_gates(logits, rb_ref[...])):
        gates_ref[0, e:e + 1, :] = row


def _norm_mod(x, gain, sc, sh, tm=512):
    b, t, d = x.shape
    tm = min(tm, t)
    per_b = sc.shape[0] > 1
    mod_map = (lambda bi, i: (bi, 0, 0)) if per_b else (lambda bi, i: (0, 0, 0))
    return pl.pallas_call(
        _norm_mod_kernel,
        grid=(b, t // tm),
        in_specs=[pl.BlockSpec((1, tm, d), lambda bi, i: (bi, i, 0)),
                  pl.BlockSpec((1, d), lambda bi, i: (0, 0)),
                  pl.BlockSpec((1, 1, d), mod_map),
                  pl.BlockSpec((1, 1, d), mod_map)],
        out_specs=pl.BlockSpec((1, tm, d), lambda bi, i: (bi, i, 0)),
        out_shape=jax.ShapeDtypeStruct((b, t, d), BF16),
        compiler_params=_params(("parallel", "parallel")),
        name="norm_mod",
    )(x, gain.reshape(1, d), sc, sh)


def _resid_kernel(x_ref, f_ref, gt_ref, xo_ref):
    xo_ref[0] = x_ref[0] + gt_ref[0] * _from_token_rows(f_ref, x_ref.shape[1])


def _resid_norm_mod_kernel(x_ref, f_ref, gt_ref, g_ref, sc_ref, sh_ref, xo_ref, o_ref):
    x = x_ref[0] + gt_ref[0] * _from_token_rows(f_ref, x_ref.shape[1])
    xo_ref[0] = x
    o_ref[0] = _normed(x, g_ref[...], sc_ref[0], sh_ref[0]).astype(BF16)


def _resid(x, f_rows, gt, norm=None, tm=512):
    b, t, d = x.shape
    tm = min(tm, t)
    nt = t // tm
    tok = lambda: pl.BlockSpec((1, tm, d), lambda bi, i: (bi, i, 0))
    per_b = lambda: pl.BlockSpec((1, 1, d), lambda bi, i: (bi, 0, 0))
    in_specs = [tok(), pl.BlockSpec((tm * ROW_TILES, LANES), lambda bi, i: (bi * nt + i, 0)), per_b()]
    args = [x, f_rows, gt]
    if norm is None:
        return pl.pallas_call(
            _resid_kernel, grid=(b, nt), in_specs=in_specs, out_specs=tok(),
            out_shape=jax.ShapeDtypeStruct((b, t, d), F32),
            compiler_params=_params(("parallel", "parallel")), name="resid",
        )(*args)
    gain, sc, sh = norm
    return pl.pallas_call(
        _resid_norm_mod_kernel, grid=(b, nt),
        in_specs=in_specs + [pl.BlockSpec((1, d), lambda bi, i: (0, 0)), per_b(), per_b()],
        out_specs=[tok(), tok()],
        out_shape=[jax.ShapeDtypeStruct((b, t, d), F32), jax.ShapeDtypeStruct((b, t, d), BF16)],
        compiler_params=_params(("parallel", "parallel")), name="resid_norm_mod",
    )(*args, gain.reshape(1, d), sc, sh)


def _norm_mod_router(x, gain, sc, sh, router_w_t, router_b, tm=512):
    b, t, d = x.shape
    tm = min(tm, t)
    e = router_w_t.shape[0]
    return pl.pallas_call(
        _norm_mod_router_kernel,
        grid=(b, t // tm),
        in_specs=[pl.BlockSpec((1, tm, d), lambda bi, i: (bi, i, 0)),
                  pl.BlockSpec((1, d), lambda bi, i: (0, 0)),
                  pl.BlockSpec((1, 1, d), lambda bi, i: (bi, 0, 0)),
                  pl.BlockSpec((1, 1, d), lambda bi, i: (bi, 0, 0)),
                  pl.BlockSpec((e, d), lambda bi, i: (0, 0)),
                  pl.BlockSpec((e, 1), lambda bi, i: (0, 0))],
        out_specs=[pl.BlockSpec((1, tm * ROW_TILES, LANES), lambda bi, i: (bi, i, 0)),
                   pl.BlockSpec((1, e, tm), lambda bi, i: (bi, 0, i))],
        out_shape=[jax.ShapeDtypeStruct((b, t * ROW_TILES, LANES), F32),
                   jax.ShapeDtypeStruct((b, e, t), F32)],
        compiler_params=_params(("parallel", "parallel")),
        name="norm_mod_router",
    )(x, gain.reshape(1, d), sc, sh, router_w_t, router_b.reshape(e, 1))


def _mm_kernel(*refs, n_terms, a_batched, w_batched, has_res):
    a_refs = refs[:n_terms]
    w_refs = refs[n_terms:2 * n_terms]
    o_ref = refs[-1]
    acc = None
    for a_ref, w_ref, ab, wb in zip(a_refs, w_refs, a_batched, w_batched):
        a = a_ref[0] if ab else a_ref[...]
        w = w_ref[0] if wb else w_ref[...]
        p = jnp.dot(a, w, preferred_element_type=F32)
        acc = p if acc is None else acc + p
    if has_res:
        res_ref, gate_ref = refs[2 * n_terms], refs[2 * n_terms + 1]
        acc = res_ref[0] + gate_ref[0] * acc
    o_ref[0] = acc.astype(o_ref.dtype)


def _matmul(a_list, w_list, out_dtype, *, batch, rows, tm, tn, res=None, gate=None, name="matmul"):
    n = w_list[0].shape[-1]
    tm = min(tm, rows)
    tn = min(tn, n)
    a_batched = tuple(a.ndim == 3 for a in a_list)
    w_batched = tuple(w.ndim == 3 for w in w_list)
    in_specs = []
    for a, ab in zip(a_list, a_batched):
        k = a.shape[-1]
        in_specs.append(pl.BlockSpec((1, tm, k), lambda b, i, j: (b, i, 0)) if ab
                        else pl.BlockSpec((tm, k), lambda b, i, j: (i, 0)))
    for w, wb in zip(w_list, w_batched):
        k = w.shape[-2]
        in_specs.append(pl.BlockSpec((1, k, tn), lambda b, i, j: (b, 0, j)) if wb
                        else pl.BlockSpec((k, tn), lambda b, i, j: (0, j)))
    args = list(a_list) + list(w_list)
    if res is not None:
        in_specs.append(pl.BlockSpec((1, tm, tn), lambda b, i, j: (b, i, j)))
        in_specs.append(pl.BlockSpec((1, 1, tn), lambda b, i, j: (b, 0, j)))
        args += [res, gate]
    kern = functools.partial(_mm_kernel, n_terms=len(a_list), a_batched=a_batched, w_batched=w_batched,
                             has_res=res is not None)
    return pl.pallas_call(
        kern,
        grid=(batch, rows // tm, n // tn),
        in_specs=in_specs,
        out_specs=pl.BlockSpec((1, tm, tn), lambda b, i, j: (b, i, j)),
        out_shape=jax.ShapeDtypeStruct((batch, rows, n), out_dtype),
        compiler_params=_params(("parallel", "parallel", "parallel")),
        name=name,
    )(*args)


def _pair_ones():
    r = lax.broadcasted_iota(jnp.int32, (LANES, LANES), 0) // RWKV_HEAD
    c = lax.broadcasted_iota(jnp.int32, (LANES, LANES), 1) // RWKV_HEAD
    return jnp.where(r == c, 1.0, 0.0).astype(BF16)


def _head_sum(x, ones_bf16):
    return _dot_exact_rhs(x, ones_bf16)


def _token_shift(ref, s, n, t, mu_p, mu_n):
    x = ref[0, s:s + n, :]
    w = x.shape[-1]
    ridx = lax.broadcasted_iota(jnp.int32, (n, w), 0)
    prev_row = ref[0, s - 1:s, :] if s > 0 else jnp.zeros((1, w), F32)
    next_row = ref[0, s + n:s + n + 1, :] if s + n < t else jnp.zeros((1, w), F32)
    prev = jnp.where(ridx == 0, prev_row, pltpu.roll(x, 1, 0))
    nxt = jnp.where(ridx == n - 1, next_row, pltpu.roll(x, n - 1, 0))
    return x + mu_p * (prev - x) + mu_n * (nxt - x)


def _rwkv_prep_kernel(pr_ref, pk_ref, pv_ref, pl_ref, par_ref, lpar_ref, w2_ref, a2_ref, g2_ref,
                      r_ref, v_ref, kk_ref, g_ref, ks_ref, lw_ref, kd_ref, alr_ref, *, t, rows):
    ones = _pair_ones()
    par = par_ref[...]
    lpar = lpar_ref[...]
    k_k, k_a = par[10:11], par[11:12]
    for s in range(0, t, rows):
        sl = slice(s, s + rows)
        r = _token_shift(pr_ref, s, rows, t, par[0:1], par[1:2])
        k = _token_shift(pk_ref, s, rows, t, par[2:3], par[3:4])
        v = _token_shift(pv_ref, s, rows, t, par[4:5], par[5:6])
        lo = _token_shift(pl_ref, s, rows, t, lpar[0:1], lpar[1:2])
        xwa, xg = lo[:, :LANES], lo[:, LANES:]
        r_ref[0, sl, :] = r
        v_ref[0, sl, :] = v
        g_ref[0, sl, :] = _dot3(_sigmoid(xg), g2_ref[...])
        kk = k * k_k
        kk = kk * lax.rsqrt(_head_sum(kk * kk, ones) + KK_EPS)
        kk_ref[0, sl, :] = kk
        tw = jnp.tanh(xwa)
        xa = xwa
        ksum = None
        for d in range(2):
            z = -(par[6 + d:7 + d] + _dot3(tw, w2_ref[d]))
            softplus = jnp.maximum(z, 0.0) + jnp.log(1.0 + jnp.exp(-jnp.abs(z)))
            w_log = -softplus - 0.5
            lw_ref[d, 0, sl, :] = -jnp.exp(w_log)
            a = _sigmoid(par[8 + d:9 + d] + _dot3(xa, a2_ref[d]))
            alr_ref[d, 0, sl, :] = a
            k_d = k * (1.0 + (a - 1.0) * k_a)
            kd_ref[d, 0, sl, :] = k_d
            ksum = k_d if ksum is None else ksum + k_d
        ks_ref[0, sl, :] = ksum


def _rwkv_prep(pr, pk, pv, plo, par, lpar, w2, a2, g2):
    b, t, _ = pr.shape
    npair = D_A // LANES
    rows = min(256, t)
    lora = plo.shape[-1]
    slab = lambda: pl.BlockSpec((1, t, LANES), lambda bi, p: (bi, 0, p))
    dslab = lambda: pl.BlockSpec((2, 1, t, LANES), lambda bi, p: (0, bi, 0, p))
    one = jax.ShapeDtypeStruct((b, t, D_A), F32)
    two = jax.ShapeDtypeStruct((2, b, t, D_A), F32)
    return pl.pallas_call(
        functools.partial(_rwkv_prep_kernel, t=t, rows=rows),
        grid=(b, npair),
        in_specs=[slab(), slab(), slab(),
                  pl.BlockSpec((1, t, lora), lambda bi, p: (bi, 0, 0)),
                  pl.BlockSpec((16, LANES), lambda bi, p: (0, p)),
                  pl.BlockSpec((8, lora), lambda bi, p: (0, 0)),
                  pl.BlockSpec((2, LANES, LANES), lambda bi, p: (0, 0, p)),
                  pl.BlockSpec((2, LANES, LANES), lambda bi, p: (0, 0, p)),
                  pl.BlockSpec((G_LORA, LANES), lambda bi, p: (0, p))],
        out_specs=[slab(), slab(), slab(), slab(), slab(), dslab(), dslab(), dslab()],
        out_shape=[one, one, one, one, one, two, two, two],
        compiler_params=_params(("parallel", "parallel")),
        name="rwkv_prep",
    )(pr, pk, pv, plo, par, lpar, w2, a2, g2)


_gram_dot = _bdot
_inv_dot = _bdot
_wu_dot = _bdot
_state_dot = _bdot


def _wkv_local(chunks, sgn, consts):
    n = WKV_CHUNK
    n2 = 2 * n
    tri, keep, dt, eye, lvl = consts
    order = sgn * dt
    strict = order > 0
    incl = order >= 0
    tri_b = jnp.where(sgn * tri >= 0, 1.0, 0.0).astype(BF16)
    each = lambda f, *ls: [f(*xs) for xs in zip(*ls)]

    def stack(x):
        return jnp.where(keep, jnp.concatenate([x, x], axis=0), 0.0)

    lws = [c[1] for c in chunks]
    cums = each(lambda lw: _dot_exact_lhs(tri_b, lw), lws)
    ctots = each(lambda lw: jnp.sum(lw, axis=0, keepdims=True), lws)
    lhs, rhs, kh2s, bh2s, v2s, at2s, rt2s = [], [], [], [], [], [], []
    for (r, lw, k, v, kk, alr), cum, ctot in zip(chunks, cums, ctots):
        e_neg = jnp.exp(-cum)
        e_rem = jnp.exp(ctot - cum)
        b = kk * alr
        rt2 = stack(r * jnp.exp(cum))
        at2 = stack(-kk * jnp.exp(cum - lw))
        lhs.append(jnp.concatenate([at2, rt2], axis=0))
        rhs.append(jnp.concatenate([stack(b * e_neg), stack(k * e_neg)], axis=0))
        kh2s.append(stack(k * e_rem))
        bh2s.append(stack(b * e_rem))
        v2s.append(stack(v))
        at2s.append(at2)
        rt2s.append(rt2)
    gs = each(lambda x, y: _gram_dot(x, y, NT_DIMS), lhs, rhs)
    a_abs = [jnp.where(strict, g[:n2, :n2], 0.0) for g in gs]
    a_rbs = [jnp.where(incl, g[n2:, :n2], 0.0) for g in gs]
    a_kks = [jnp.concatenate([jnp.where(strict, g[:n2, n2:], 0.0), jnp.where(incl, g[n2:, n2:], 0.0)], axis=0)
             for g in gs]
    m0 = sgn * lvl[0] == 1
    tinvs = [eye + jnp.where(m0, a, 0.0) for a in a_abs]
    for m in lvl[1:]:
        msk = sgn * m == 1
        cts = each(lambda a, t: _inv_dot(jnp.where(msk, a, 0.0), t), a_abs, tinvs)
        tinvs = each(lambda t, ct: t + _inv_dot(t, ct), tinvs, cts)
    kvs = each(_bdot, a_kks, v2s)
    wus = each(lambda t, at2, kv: _wu_dot(t, jnp.concatenate([at2, kv[:n2]], axis=1)), tinvs, at2s, kvs)
    rys = each(lambda rt2, kv, a_rb, wu: jnp.concatenate([rt2, kv[n2:]], axis=1) + _bdot(a_rb, wu),
               rt2s, kvs, a_rbs, wus)
    bws = each(lambda bh2, wu: _bdot(bh2, wu, TN_DIMS), bh2s, wus)
    khv = each(lambda kh2, v2: _bdot(kh2, v2, TN_DIMS), kh2s, v2s)
    out = []
    for ry, bw, kv, ctot in zip(rys, bws, khv, ctots):
        decay_col = jnp.sum(jnp.where(eye > 0, jnp.exp(ctot), 0.0), axis=1, keepdims=True)
        out.append((ry[:, :LANES], ry[:, LANES:], bw[:, :LANES], bw[:, LANES:] + kv, decay_col))
    return out


def _wkv_consts():
    n = WKV_CHUNK
    n2 = 2 * n
    tri = lax.broadcasted_iota(jnp.int32, (n, n), 0) - lax.broadcasted_iota(jnp.int32, (n, n), 1)
    lane = lax.broadcasted_iota(jnp.int32, (n2, LANES), 1)
    rr = lax.broadcasted_iota(jnp.int32, (n2, LANES), 0)
    keep = (lane >= RWKV_HEAD) == (rr >= n)
    ri = lax.broadcasted_iota(jnp.int32, (n2, n2), 0)
    ci = lax.broadcasted_iota(jnp.int32, (n2, n2), 1)
    r2 = ri & (n - 1)
    c2 = ci & (n - 1)
    dt = r2 - c2
    eye = jnp.where(ri == ci, 1.0, 0.0).astype(F32)
    lvl = []
    k = 0
    while (1 << k) < n:
        same = (r2 >> (k + 1)) == (c2 >> (k + 1))
        diff = ((r2 >> k) & 1) - ((c2 >> k) & 1)
        lvl.append(jnp.where(same, diff, 0))
        k += 1
    return tri, keep, dt, eye, lvl


def _wkv_kernel(r_ref, v_ref, kk_ref, lw_ref, kd_ref, alr_ref, s0_ref, y_ref, sfin_ref, st_ref, *, n_sub):
    d = pl.program_id(0)
    c = pl.program_id(3)
    nc = pl.num_programs(3)

    @pl.when(c == 0)
    def _():
        st_ref[...] = s0_ref[0, 0, 0]

    sgn = 1 - 2 * d
    rows = []
    for u in range(n_sub):
        ui = u + d * (n_sub - 1 - 2 * u)
        rows.append(pl.ds(pl.multiple_of(ui * WKV_CHUNK, WKV_CHUNK), WKV_CHUNK))
    chunks = [(r_ref[0, rw, :], lw_ref[0, 0, rw, :], kd_ref[0, 0, rw, :], v_ref[0, rw, :], kk_ref[0, rw, :],
               alr_ref[0, 0, rw, :]) for rw in rows]
    local = _wkv_local(chunks, sgn, _wkv_consts())
    st = st_ref[...]
    for rw, (rhat, yhat, mmat, nmat, decay_col) in zip(rows, local):
        y2 = _state_dot(rhat, st) + yhat
        y_ref[0, 0, rw, :] = y2[:WKV_CHUNK] + y2[WKV_CHUNK:]
        st = decay_col * st + _state_dot(mmat, st) + nmat
    st_ref[...] = st

    @pl.when(c == nc - 1)
    def _():
        sfin_ref[0, 0, 0] = st


def _wkv(r, v, kk, lw, kd, alr, s0):
    b, t, _ = r.shape
    npair = D_A // LANES
    n_sub = min(WKV_SUB, t // WKV_CHUNK)
    blk = n_sub * WKV_CHUNK
    nc = t // blk

    def tmap(d, bi, p, c):
        return (bi, c + d * (nc - 1 - 2 * c), p)

    def dmap(d, bi, p, c):
        return (d, bi, c + d * (nc - 1 - 2 * c), p)

    smap = lambda d, bi, p, c: (d, bi, p, 0, 0)
    shared = lambda: pl.BlockSpec((1, blk, LANES), tmap)
    direc = lambda: pl.BlockSpec((1, 1, blk, LANES), dmap)
    state = lambda: pl.BlockSpec((1, 1, 1, LANES, LANES), smap)
    return pl.pallas_call(
        functools.partial(_wkv_kernel, n_sub=n_sub),
        grid=(2, b, npair, nc),
        in_specs=[shared(), shared(), shared(), direc(), direc(), direc(), state()],
        out_specs=[direc(), state()],
        out_shape=[jax.ShapeDtypeStruct((2, b, t, D_A), F32),
                   jax.ShapeDtypeStruct((2, b, npair, LANES, LANES), F32)],
        scratch_shapes=[pltpu.VMEM((LANES, LANES), F32)],
        compiler_params=_params(("parallel", "parallel", "parallel", "arbitrary")),
        name="wkv",
    )(r, v, kk, lw, kd, alr, s0)


def _rwkv_finish_kernel(y_ref, r_ref, v_ref, ks_ref, g_ref, par_ref, o_ref):
    ones = _pair_ones()
    par = par_ref[...]
    inv_n = 1.0 / RWKV_HEAD
    for p in range(D_A // LANES):
        sl = slice(p * LANES, (p + 1) * LANES)
        y = y_ref[0, 0, :, sl] + y_ref[1, 0, :, sl]
        mu = _head_sum(y, ones) * inv_n
        yc = y - mu
        var = _head_sum(yc * yc, ones) * inv_n
        yn = yc * lax.rsqrt(var + LNX_EPS) * par[0:1, sl] + par[1:2, sl]
        bonus = _head_sum(r_ref[0, :, sl] * ks_ref[0, :, sl] * par[2:3, sl], ones) * v_ref[0, :, sl]
        o_ref[0, :, sl] = ((yn + bonus) * g_ref[0, :, sl]).astype(BF16)


def _rwkv_finish(y, r, v, ks, g, par, tm=256):
    b, t, d = r.shape
    tile = lambda: pl.BlockSpec((1, tm, d), lambda bi, i: (bi, i, 0))
    return pl.pallas_call(
        _rwkv_finish_kernel,
        grid=(b, t // tm),
        in_specs=[pl.BlockSpec((2, 1, tm, d), lambda bi, i: (0, bi, i, 0)), tile(), tile(), tile(), tile(),
                  pl.BlockSpec((8, d), lambda bi, i: (0, 0))],
        out_specs=tile(),
        out_shape=jax.ShapeDtypeStruct((b, t, d), BF16),
        compiler_params=_params(("parallel", "parallel")),
        name="rwkv_finish",
    )(y, r, v, ks, g, par)


def _head_rms(x, gain):
    return x * lax.rsqrt(jnp.mean(x * x, axis=-1, keepdims=True) + NORM_EPS) * gain


def _rope(x, cos_e, sin_s):
    lane = lax.broadcasted_iota(jnp.int32, x.shape, 1)
    swapped = jnp.where((lane & 1) == 0, pltpu.roll(x, LANES - 1, 1), pltpu.roll(x, 1, 1))
    return x * cos_e + swapped * sin_s


def _attn_prep_kernel(*refs, with_q):
    if with_q:
        pq_ref, pkv_ref, gains_ref, cos_ref, sin_ref, q_ref, k_ref, v_ref = refs
    else:
        pkv_ref, gains_ref, k_ref, v_ref = refs
    gains = gains_ref[...]
    kv_w = H_KV * HEAD_DIM
    for h in range(H_KV):
        sl = slice(h * HEAD_DIM, (h + 1) * HEAD_DIM)
        kh = _head_rms(pkv_ref[0, :, sl], gains[1:2])
        if with_q:
            kh = _rope(kh, cos_ref[...], sin_ref[...])
        k_ref[0, :, sl] = kh.astype(BF16)
    v_ref[0] = pkv_ref[0, :, kv_w:].astype(BF16)
    if with_q:
        scale = HEAD_DIM ** -0.5
        for h in range(H_Q):
            sl = slice(h * HEAD_DIM, (h + 1) * HEAD_DIM)
            qh = _rope(_head_rms(pq_ref[0, :, sl], gains[0:1]), cos_ref[...], sin_ref[...])
            q_ref[0, :, sl] = (qh * scale).astype(BF16)


def _attn_prep(pq, pkv, gains, cos_e, sin_s, tm=256):
    b, t, _ = pkv.shape
    with_q = pq is not None
    kv_w = H_KV * HEAD_DIM
    tm = min(tm, t)
    kv_spec = lambda: pl.BlockSpec((1, tm, kv_w), lambda bi, i: (bi, i, 0))
    in_specs = [pl.BlockSpec((1, tm, 2 * kv_w), lambda bi, i: (bi, i, 0)),
                pl.BlockSpec((8, HEAD_DIM), lambda bi, i: (0, 0))]
    args = [pkv, gains]
    out_specs = [kv_spec(), kv_spec()]
    out_shape = [jax.ShapeDtypeStruct((b, t, kv_w), BF16)] * 2
    if with_q:
        in_specs = [pl.BlockSpec((1, tm, D_B), lambda bi, i: (bi, i, 0))] + in_specs + [
            pl.BlockSpec((tm, HEAD_DIM), lambda bi, i: (i, 0)),
            pl.BlockSpec((tm, HEAD_DIM), lambda bi, i: (i, 0))]
        args = [pq] + args + [cos_e, sin_s]
        out_specs = [pl.BlockSpec((1, tm, D_B), lambda bi, i: (bi, i, 0))] + out_specs
        out_shape = [jax.ShapeDtypeStruct((b, t, D_B), BF16)] + out_shape
    return pl.pallas_call(
        functools.partial(_attn_prep_kernel, with_q=with_q),
        grid=(b, t // tm),
        in_specs=in_specs,
        out_specs=out_specs,
        out_shape=out_shape,
        compiler_params=_params(("parallel", "parallel")),
        name="attn_prep",
    )(*args)


def _attn_kernel(q_ref, k_ref, v_ref, o_ref):
    k = k_ref[0]
    v = v_ref[0]
    for g in range(Q_PER_KV):
        sl = slice(g * HEAD_DIM, (g + 1) * HEAD_DIM)
        s = lax.dot_general(q_ref[0, :, sl], k, NT_DIMS, preferred_element_type=F32)
        p = jnp.exp(s - jnp.max(s, axis=-1, keepdims=True))
        l = jnp.sum(p, axis=-1, keepdims=True)
        o = jnp.dot(p.astype(BF16), v, preferred_element_type=F32)
        o_ref[0, :, sl] = (o / l).astype(BF16)


def _attention(q, k_all, v_all, tq=256):
    b, t, _ = q.shape
    s_len = k_all.shape[1]
    gw = Q_PER_KV * HEAD_DIM
    return pl.pallas_call(
        _attn_kernel,
        grid=(b, H_KV, t // tq),
        in_specs=[pl.BlockSpec((1, tq, gw), lambda bi, h, i: (bi, i, h)),
                  pl.BlockSpec((1, s_len, HEAD_DIM), lambda bi, h, i: (bi, 0, h)),
                  pl.BlockSpec((1, s_len, HEAD_DIM), lambda bi, h, i: (bi, 0, h))],
        out_specs=pl.BlockSpec((1, tq, gw), lambda bi, h, i: (bi, i, h)),
        out_shape=jax.ShapeDtypeStruct((b, t, H_Q * HEAD_DIM), BF16),
        compiler_params=_params(("parallel", "parallel", "parallel")),
        name="attention",
    )(q, k_all, v_all)


def _dft_channels_kernel(h_ref, w_ref, c_ref, s_ref):
    p = jnp.dot(h_ref[0], w_ref[...], preferred_element_type=F32)
    c_ref[0] = p[:, :FOURIER_GROUP_DIM].astype(BF16)
    s_ref[0] = p[:, FOURIER_GROUP_DIM:].astype(BF16)


def _dft_channels(h, w_cs, tm=512):
    b, t, d = h.shape
    tm = min(tm, t)
    gd = FOURIER_GROUP_DIM
    blk = lambda: pl.BlockSpec((1, tm, gd), lambda bi, i, g: (bi, i, g))
    return pl.pallas_call(
        _dft_channels_kernel,
        grid=(b, t // tm, d // gd),
        in_specs=[blk(), pl.BlockSpec((gd, 2 * gd), lambda bi, i, g: (0, 0))],
        out_specs=[blk(), blk()],
        out_shape=[jax.ShapeDtypeStruct((b, t, d), BF16)] * 2,
        compiler_params=_params(("parallel", "parallel", "parallel")),
        name="dft_channels",
    )(h, w_cs)


def _dft_mats(n):
    i = lax.broadcasted_iota(jnp.int32, (n, n), 0)
    j = lax.broadcasted_iota(jnp.int32, (n, n), 1)
    ang = ((i * j) % n).astype(F32) * (2.0 * np.pi / n)
    scale = n ** -0.5
    return jnp.cos(ang) * scale, jnp.sin(ang) * scale


MOE_TILE = 512


def _moe_kernel(src_ref, grp_ref, live_ref, h_hbm, gs_ref, wg_ref, wu_ref, wd_ref, o_hbm,
                rows_ref, h16_ref, acc_ref, gsem, ssem, *, tm, n_tok):
    i = pl.program_id(0)
    j = pl.program_id(1)
    n_live = live_ref[0]
    live = i < n_live
    slot = i % 2

    def row(ref, t):
        return ref.at[pl.ds(pl.multiple_of(t * ROW_TILES, ROW_TILES), ROW_TILES)]

    def start_gather(tile, sl):
        def issue(k, c):
            tok = jnp.maximum(src_ref[tile * tm + k], 0)
            pltpu.make_async_copy(row(h_hbm, tok), row(rows_ref.at[sl], k), gsem.at[sl]).start()
            return c
        lax.fori_loop(0, tm, issue, 0, unroll=8)

    def wait_gather(sl):
        pltpu.make_async_copy(h_hbm.at[pl.ds(0, tm * ROW_TILES)], rows_ref.at[sl], gsem.at[sl]).wait()

    def start_scatter(tile, sl):
        def issue(k, c):
            tok = src_ref[tile * tm + k]
            tok = jnp.where(tok < 0, n_tok + k, tok)
            pltpu.make_async_copy(row(rows_ref.at[sl], k), row(o_hbm, tok), ssem.at[sl]).start()
            return c
        lax.fori_loop(0, tm, issue, 0, unroll=8)

    def wait_scatter(sl):
        pltpu.make_async_copy(rows_ref.at[sl], o_hbm.at[pl.ds(0, tm * ROW_TILES)], ssem.at[sl]).wait()

    @pl.when(jnp.logical_and(live, j == 0))
    def _():
        @pl.when(i == 0)
        def _():
            start_gather(0, 0)
            rows_ref[1] = jnp.zeros(rows_ref.shape[1:], F32)
            spare = pltpu.make_async_copy(rows_ref.at[1], o_hbm.at[pl.ds(n_tok * ROW_TILES, tm * ROW_TILES)],
                                          ssem.at[1])
            spare.start()
            spare.wait()
        wait_gather(slot)
        h16_ref[...] = _from_token_rows(rows_ref, tm, lead=(slot,)).astype(BF16)
        acc_ref[...] = jnp.zeros_like(acc_ref)

    @pl.when(jnp.logical_and(live, j == 1))
    def _():
        @pl.when(i > 0)
        def _():
            wait_scatter(1 - slot)

        @pl.when(i + 1 < n_live)
        def _():
            start_gather(i + 1, 1 - slot)

    @pl.when(live)
    def _():
        h = h16_ref[...]
        u = _silu(_bdot(h, wg_ref[0, 0])) * _bdot(h, wu_ref[0, 0])
        lane = lax.broadcasted_iota(jnp.int32, gs_ref.shape, 1)
        gcol = jnp.sum(jnp.where(lane == j, gs_ref[...], 0.0), axis=1, keepdims=True)
        acc_ref[...] += gcol * _bdot(u, wd_ref[0, 0])

    @pl.when(jnp.logical_and(live, j == pl.num_programs(1) - 1))
    def _():
        _to_token_rows(rows_ref, acc_ref[...], lead=(slot,))
        start_scatter(i, slot)

        @pl.when(i == n_live - 1)
        def _():
            wait_scatter(slot)


def _route_meta(gates, tm):
    n = gates.shape[0]
    ng, epg = N_EXPERT_GROUPS, EXPERTS_PER_GROUP
    gg = gates.reshape(n, ng, epg)
    grp = jnp.argmax(jnp.max(gg, axis=-1) > 0, axis=-1).astype(jnp.int32)
    onehot = (grp[:, None] == jnp.arange(ng)[None, :]).astype(jnp.int32)
    counts = jnp.sum(onehot, axis=0)
    padded = ((counts + tm - 1) // tm) * tm
    ends = jnp.cumsum(padded)
    starts = ends - padded
    rank = jnp.sum((jnp.cumsum(onehot, axis=0) - onehot) * onehot, axis=1)
    pos = jnp.sum(starts[None, :] * onehot, axis=1) + rank
    p_total = n + ng * tm
    src = jnp.full((p_total,), -1, jnp.int32).at[pos].set(jnp.arange(n, dtype=jnp.int32))
    gsel = jnp.sum(gg * onehot[:, :, None].astype(F32), axis=1)
    gs_sorted = jnp.zeros((p_total, epg), F32).at[pos].set(gsel)
    n_live = ends[-1] // tm
    tile_start = jnp.minimum(jnp.arange(p_total // tm, dtype=jnp.int32), n_live - 1) * tm
    tile_grp = jnp.minimum(jnp.sum(tile_start[:, None] >= ends[None, :], axis=1), ng - 1).astype(jnp.int32)
    return src, gs_sorted, tile_grp, n_live.astype(jnp.int32).reshape(1)


def _moe(h_rows, gates, wg, wu, wd, layer, tm=MOE_TILE):
    n = gates.shape[0]
    d = D_MODEL
    _, ne, _, de = wg.shape
    epg = EXPERTS_PER_GROUP
    src, gs_sorted, tile_grp, n_live = _route_meta(gates, tm)
    n_tiles = src.shape[0] // tm

    def wmap(i, j, src, grp, live):
        return (layer, grp[i] * epg + jnp.where(i < live[0], j, epg - 1), 0, 0)

    grid_spec = pltpu.PrefetchScalarGridSpec(
        num_scalar_prefetch=3,
        grid=(n_tiles, epg),
        in_specs=[pl.BlockSpec(memory_space=pl.ANY),
                  pl.BlockSpec((tm, epg), lambda i, j, src, grp, live: (i, 0)),
                  pl.BlockSpec((1, 1, d, de), wmap),
                  pl.BlockSpec((1, 1, d, de), wmap),
                  pl.BlockSpec((1, 1, de, d), wmap)],
        out_specs=pl.BlockSpec(memory_space=pl.ANY),
        scratch_shapes=[pltpu.VMEM((2, tm * ROW_TILES, LANES), F32), pltpu.VMEM((tm, d), BF16),
                        pltpu.VMEM((tm, d), F32), pltpu.SemaphoreType.DMA((2,)), pltpu.SemaphoreType.DMA((2,))],
    )
    return pl.pallas_call(
        functools.partial(_moe_kernel, tm=tm, n_tok=n),
        grid_spec=grid_spec,
        out_shape=jax.ShapeDtypeStruct(((n + tm) * ROW_TILES, LANES), F32),
        compiler_params=pltpu.CompilerParams(dimension_semantics=("arbitrary", "arbitrary"),
                                             vmem_limit_bytes=VMEM_LIMIT, disable_bounds_checks=True),
        name="moe",
    )(src, tile_grp, n_live, h_rows, gs_sorted, wg, wu, wd)


def _rope_tables(n):
    rows = n // GRID_W
    row = jnp.repeat(jnp.arange(rows), GRID_W).astype(F32)
    col = jnp.tile(jnp.arange(GRID_W), rows).astype(F32)
    inv = ROPE_THETA ** (-jnp.arange(0, ROPE_AXIS_DIM, 2, dtype=F32) / ROPE_AXIS_DIM)
    ang = jnp.concatenate([row[:, None] * inv, col[:, None] * inv], axis=-1)
    cos_e = jnp.repeat(jnp.cos(ang), 2, axis=-1)
    sin_s = jnp.stack([-jnp.sin(ang), jnp.sin(ang)], axis=-1).reshape(n, HEAD_DIM)
    return cos_e, sin_s


def _pad_rows(rows, n):
    a = jnp.stack(rows, axis=0)
    return jnp.pad(a, ((0, n - a.shape[0]), (0, 0)))


def _enter_layer(x, pending, gain, sc, sh):
    if pending is None:
        return x, _norm_mod(x, gain, sc, sh)
    f_rows, gt = pending
    return _resid(x, f_rows, gt, norm=(gain, sc, sh))


def _even_mixer(x, pending, ctx, mods, gain, w_in, mu_prev, mu_next, w0, w2, a0, a2, g2, k_k, k_a, r_k, ln_g, ln_b,
                q_gain, k_gain, w_out):
    bsz, seq, _ = x.shape
    sh1, sc1, gt1, csh1, csc1 = mods
    x, h_lat = _enter_layer(x, pending, gain, sc1, sh1)
    h_ctx = _norm_mod(ctx, gain, csc1, csh1)

    a_cols = 3 * D_A + W_LORA + A_LORA + G_LORA
    wb = w_in.astype(BF16)
    w_r, w_k, w_v = wb[:, :D_A], wb[:, D_A:2 * D_A], wb[:, 2 * D_A:3 * D_A]
    w_lo = wb[:, 3 * D_A:a_cols]
    w_q = wb[:, a_cols:a_cols + D_B]
    w_kv = wb[:, a_cols + D_B:]

    def proj(h, w, name):
        rows = h.shape[1]
        return _matmul([h], [w], F32, batch=bsz, rows=rows, tm=1024, tn=512, name=name)

    par = _pad_rows([mu_prev[:D_A], mu_next[:D_A], mu_prev[D_A:2 * D_A], mu_next[D_A:2 * D_A],
                     mu_prev[2 * D_A:3 * D_A], mu_next[2 * D_A:3 * D_A],
                     w0[0], w0[1], a0[0], a0[1], k_k, k_a], 16)
    lpar = _pad_rows([mu_prev[3 * D_A:], mu_next[3 * D_A:]], 8)
    fin_par = _pad_rows([ln_g, ln_b, r_k.reshape(-1)], 8)
    w2 = jnp.pad(w2, ((0, 0), (0, A_LORA), (0, 0)))
    a2 = jnp.pad(a2, ((0, 0), (W_LORA, 0), (0, 0)))

    def streams(h):
        return _rwkv_prep(proj(h, w_r, "proj_r"), proj(h, w_k, "proj_k"), proj(h, w_v, "proj_v"),
                          proj(h, w_lo, "proj_lora"), par, lpar, w2, a2, g2)

    rc, vc, kkc, _, _, lwc, kdc, alrc = streams(h_ctx)
    rl, vl, kkl, gl, ksl, lwl, kdl, alrl = streams(h_lat)
    s0 = jnp.zeros((2, bsz, D_A // LANES, LANES, LANES), F32)
    _, s_ctx = _wkv(rc, vc, kkc, lwc, kdc, alrc, s0)
    y, _ = _wkv(rl, vl, kkl, lwl, kdl, alrl, s_ctx)
    ya = _rwkv_finish(y, rl, vl, ksl, gl, fin_par)

    cos_e, sin_s = _rope_tables(seq)
    gains = _pad_rows([q_gain, k_gain], 8)
    k_c, v_c = _attn_prep(None, proj(h_ctx, w_kv, "proj_kv"), gains, None, None)
    q_l, k_l, v_l = _attn_prep(proj(h_lat, w_q, "proj_q"), proj(h_lat, w_kv, "proj_kv"), gains, cos_e, sin_s)
    yb = _attention(q_l, jnp.concatenate([k_c, k_l], axis=1), jnp.concatenate([v_c, v_l], axis=1))

    wo = w_out.astype(BF16)
    return _matmul([ya, yb], [wo[:D_A], wo[D_A:]], F32, batch=bsz, rows=seq, tm=1024, tn=512,
                   res=x, gate=gt1, name="mixer_out")


def _fourier_mixer(x, pending, mods, gain, w_out):
    bsz, seq, d = x.shape
    sh1, sc1, gt1 = mods
    x, h = _enter_layer(x, pending, gain, sc1, sh1)
    cc, sc = _dft_mats(FOURIER_GROUP_DIM)
    ct, st = _dft_mats(seq)
    pc, ps = _dft_channels(h, jnp.concatenate([cc, sc], axis=1).astype(BF16))
    f = _matmul([ct.astype(BF16), (-st).astype(BF16)], [pc, ps], BF16, batch=bsz, rows=seq, tm=512, tn=512,
                name="dft_tokens")
    return _matmul([f], [w_out.astype(BF16)], F32, batch=bsz, rows=seq, tm=1024, tn=512, res=x, gate=gt1,
                   name="mixer_out")


def _moe_layer(x, mods, gain, router_w_t, router_b, w_gate, w_up, w_down, layer):
    bsz, seq, d = x.shape
    sh2, sc2, _ = mods
    h_rows, gates_t = _norm_mod_router(x, gain, sc2, sh2, router_w_t, router_b)
    gates = jnp.swapaxes(gates_t, 1, 2).reshape(bsz * seq, -1)
    return _moe(h_rows.reshape(bsz * seq * ROW_TILES, LANES), gates, w_gate, w_up, w_down, layer)


def kernel(x, c, ctx, c_ctx, mod_w, mod_b, norm_mix, norm_ffn, ab_w_in, ab_mu_prev, ab_mu_next,
           rwkv_w0, rwkv_w2, rwkv_a0, rwkv_a2, rwkv_g2, rwkv_k_k, rwkv_k_a, rwkv_r_k, rwkv_ln_g, rwkv_ln_b,
           attn_q_norm, attn_k_norm, ab_w_out, fourier_w_out, router_w, router_b,
           exp_w_gate, exp_w_up, exp_w_down):
    bsz, seq, d = x.shape
    depth = mod_w.shape[0]
    cvec = jnp.concatenate([c, c_ctx[None], jnp.zeros((8 - bsz - 1, d), F32)], axis=0)
    mod = _modulation(cvec, mod_w, mod_b)
    router_w_t = router_w.T
    pending = None
    for l in range(depth):
        m_lat = mod[l, :bsz, None, :]
        sh1, sc1, gt1, sh2, sc2, gt2 = jnp.split(m_lat, 6, axis=-1)
        if l % 2 == 0:
            e = l // 2
            m_ctx = mod[l, bsz:bsz + 1, None, :]
            csh1, csc1 = m_ctx[..., :d], m_ctx[..., d:2 * d]
            x = _even_mixer(x, pending, ctx, (sh1, sc1, gt1, csh1, csc1), norm_mix[l], ab_w_in[e], ab_mu_prev[e],
                            ab_mu_next[e], rwkv_w0[e], rwkv_w2[e], rwkv_a0[e], rwkv_a2[e], rwkv_g2[e],
                            rwkv_k_k[e], rwkv_k_a[e], rwkv_r_k[e], rwkv_ln_g[e], rwkv_ln_b[e],
                            attn_q_norm[e], attn_k_norm[e], ab_w_out[e])
        else:
            x = _fourier_mixer(x, pending, (sh1, sc1, gt1), norm_mix[l], fourier_w_out[l // 2])
        f_rows = _moe_layer(x, (sh2, sc2, gt2), norm_ffn[l], router_w_t, router_b,
                            exp_w_gate, exp_w_up, exp_w_down, l)
        pending = (f_rows, gt2)
    return _resid(x, pending[0], pending[1])
```

```python
import functools

import jax
import jax.numpy as jnp
import numpy as np
from jax import lax
from jax.experimental import pallas as pl
from jax.experimental.pallas import tpu as pltpu

F32 = jnp.float32
BF16 = jnp.bfloat16

D_MODEL = 2048
D_A = 1024
RWKV_HEAD = 64
W_LORA = 64
A_LORA = 64
G_LORA = 128
LNX_EPS = 64e-5
D_B = 1024
HEAD_DIM = 128
H_Q = 8
H_KV = 2
Q_PER_KV = H_Q // H_KV
GRID_W = 64
ROPE_THETA = 10000.0
ROPE_AXIS_DIM = HEAD_DIM // 2
FOURIER_GROUPS = 8
FOURIER_GROUP_DIM = D_MODEL // FOURIER_GROUPS
N_EXPERTS = 16
N_EXPERT_GROUPS = 4
EXPERTS_PER_GROUP = 4
D_EXPERT = 512
NORM_EPS = 1e-6
KK_EPS = 1e-12

LANES = 128
WKV_CHUNK = 64
WKV_SUB = 16
VMEM_LIMIT = 56 * 1024 * 1024

NT_DIMS = (((1,), (1,)), ((), ()))
TN_DIMS = (((0,), (0,)), ((), ()))


def _params(sem):
    return pltpu.CompilerParams(dimension_semantics=sem, vmem_limit_bytes=VMEM_LIMIT)


def _bdot(a, b, dims=None):
    a = a.astype(BF16)
    b = b.astype(BF16)
    if dims is None:
        return jnp.dot(a, b, preferred_element_type=F32)
    return lax.dot_general(a, b, dims, preferred_element_type=F32)


def _split2(x):
    hi = x.astype(BF16)
    lo = (x - hi.astype(F32)).astype(BF16)
    return hi, lo


def _split3(x):
    hi = x.astype(BF16)
    r1 = x - hi.astype(F32)
    mid = r1.astype(BF16)
    lo = (r1 - mid.astype(F32)).astype(BF16)
    return hi, mid, lo


def _dot3(a, b, dims=None):
    ah, al = _split2(a)
    bh, bl = _split2(b)
    return _bdot(ah, bh, dims) + (_bdot(ah, bl, dims) + _bdot(al, bh, dims))


def _dot_exact_lhs(a_bf16, x, dims=None):
    h, m, l = _split3(x)
    return _bdot(a_bf16, h, dims) + (_bdot(a_bf16, m, dims) + _bdot(a_bf16, l, dims))


def _dot_exact_rhs(x, b_bf16):
    h, l = _split2(x)
    return _bdot(h, b_bf16) + _bdot(l, b_bf16)


def _sigmoid(x):
    return 1.0 / (1.0 + jnp.exp(-x))


def _silu(x):
    return x * _sigmoid(x)


def _mod_kernel(c_ref, w_ref, b_ref, o_ref):
    s = _silu(c_ref[...])
    o_ref[0] = _bdot(s, w_ref[0]) + b_ref[0]


def _modulation(cvec, mod_w, mod_b):
    depth, d, n = mod_w.shape
    tn = 1024
    return pl.pallas_call(
        _mod_kernel,
        grid=(depth, n // tn),
        in_specs=[pl.BlockSpec((8, d), lambda l, j: (0, 0)),
                  pl.BlockSpec((1, d, tn), lambda l, j: (l, 0, j)),
                  pl.BlockSpec((1, 1, tn), lambda l, j: (l, 0, j))],
        out_specs=pl.BlockSpec((1, 8, tn), lambda l, j: (l, 0, j)),
        out_shape=jax.ShapeDtypeStruct((depth, 8, n), F32),
        compiler_params=_params(("parallel", "parallel")),
        name="modulation",
    )(cvec, mod_w, mod_b.reshape(depth, 1, n))


def _normed(x, gain, sc, sh):
    y = x * lax.rsqrt(jnp.mean(x * x, axis=-1, keepdims=True) + NORM_EPS)
    return (y * gain) * (1.0 + sc) + sh


def _norm_mod_kernel(x_ref, g_ref, sc_ref, sh_ref, o_ref):
    o_ref[0] = _normed(x_ref[0], g_ref[...], sc_ref[0], sh_ref[0]).astype(BF16)


def _top2_gates(logits, bias):
    s = _sigmoid(logits)
    sel = s + bias
    neg = jnp.float32(-jnp.inf)
    flags, scores = [], []
    for g in range(N_EXPERT_GROUPS):
        xs = [sel[g * EXPERTS_PER_GROUP + i:g * EXPERTS_PER_GROUP + i + 1] for i in range(EXPERTS_PER_GROUP)]
        m1 = functools.reduce(jnp.maximum, xs)
        first, taken = [], None
        for x in xs:
            f = (x == m1) if taken is None else jnp.logical_and(x == m1, jnp.logical_not(taken))
            taken = f if taken is None else jnp.logical_or(taken, f)
            first.append(f)
        rest = [jnp.where(f, neg, x) for f, x in zip(first, xs)]
        m2 = functools.reduce(jnp.maximum, rest)
        second, taken = [], None
        for f, x in zip(first, xs):
            c = jnp.logical_and(x == m2, jnp.logical_not(f))
            if taken is not None:
                c = jnp.logical_and(c, jnp.logical_not(taken))
            taken = c if taken is None else jnp.logical_or(taken, c)
            second.append(c)
        flags.append([jnp.logical_or(f, c) for f, c in zip(first, second)])
        scores.append(m1 + m2)
    best = functools.reduce(jnp.maximum, scores)
    taken, rows = None, []
    for g in range(N_EXPERT_GROUPS):
        bg = (scores[g] == best) if taken is None else jnp.logical_and(scores[g] == best, jnp.logical_not(taken))
        taken = bg if taken is None else jnp.logical_or(taken, bg)
        for i in range(EXPERTS_PER_GROUP):
            e = g * EXPERTS_PER_GROUP + i
            rows.append(jnp.where(jnp.logical_and(bg, flags[g][i]), s[e:e + 1], 0.0))
    denom = functools.reduce(jnp.add, rows)
    inv = 1.0 / denom
    return [r * inv for r in rows]


ROW_TILES = D_MODEL // LANES


def _to_token_rows(ref, x, lead=()):
    n = x.shape[0]
    for s in range(ROW_TILES):
        ref[lead + (pl.ds(s, n, stride=ROW_TILES), slice(None))] = x[:, s * LANES:(s + 1) * LANES]


def _from_token_rows(ref, n, lead=()):
    return jnp.concatenate([ref[lead + (pl.ds(s, n, stride=ROW_TILES), slice(None))] for s in range(ROW_TILES)],
                           axis=-1)


def _norm_mod_router_kernel(x_ref, g_ref, sc_ref, sh_ref, rw_ref, rb_ref, o_ref, gates_ref):
    h = _normed(x_ref[0], g_ref[...], sc_ref[0], sh_ref[0])
    _to_token_rows(o_ref, h, lead=(0,))
    logits = _dot3(rw_ref[...], h, NT_DIMS)
    for e, row in enumerate(_top2_gates(logits, rb_ref[...])):
        gates_ref[0, e:e + 1, :] = row


def _norm_mod(x, gain, sc, sh, tm=512):
    b, t, d = x.shape
    tm = min(tm, t)
    per_b = sc.shape[0] > 1
    mod_map = (lambda bi, i: (bi, 0, 0)) if per_b else (lambda bi, i: (0, 0, 0))
    return pl.pallas_call(
        _norm_mod_kernel,
        grid=(b, t // tm),
        in_specs=[pl.BlockSpec((1, tm, d), lambda bi, i: (bi, i, 0)),
                  pl.BlockSpec((1, d), lambda bi, i: (0, 0)),
                  pl.BlockSpec((1, 1, d), mod_map),
                  pl.BlockSpec((1, 1, d), mod_map)],
        out_specs=pl.BlockSpec((1, tm, d), lambda bi, i: (bi, i, 0)),
        out_shape=jax.ShapeDtypeStruct((b, t, d), BF16),
        compiler_params=_params(("parallel", "parallel")),
        name="norm_mod",
    )(x, gain.reshape(1, d), sc, sh)


def _resid_kernel(x_ref, f_ref, gt_ref, xo_ref):
    xo_ref[0] = x_ref[0] + gt_ref[0] * _from_token_rows(f_ref, x_ref.shape[1])


def _resid_norm_mod_kernel(x_ref, f_ref, gt_ref, g_ref, sc_ref, sh_ref, xo_ref, o_ref):
    x = x_ref[0] + gt_ref[0] * _from_token_rows(f_ref, x_ref.shape[1])
    xo_ref[0] = x
    o_ref[0] = _normed(x, g_ref[...], sc_ref[0], sh_ref[0]).astype(BF16)


def _resid(x, f_rows, gt, norm=None, tm=512):
    b, t, d = x.shape
    tm = min(tm, t)
    nt = t // tm
    tok = lambda: pl.BlockSpec((1, tm, d), lambda bi, i: (bi, i, 0))
    per_b = lambda: pl.BlockSpec((1, 1, d), lambda bi, i: (bi, 0, 0))
    in_specs = [tok(), pl.BlockSpec((tm * ROW_TILES, LANES), lambda bi, i: (bi * nt + i, 0)), per_b()]
    args = [x, f_rows, gt]
    if norm is None:
        return pl.pallas_call(
            _resid_kernel, grid=(b, nt), in_specs=in_specs, out_specs=tok(),
            out_shape=jax.ShapeDtypeStruct((b, t, d), F32),
            compiler_params=_params(("parallel", "parallel")), name="resid",
        )(*args)
    gain, sc, sh = norm
    return pl.pallas_call(
        _resid_norm_mod_kernel, grid=(b, nt),
        in_specs=in_specs + [pl.BlockSpec((1, d), lambda bi, i: (0, 0)), per_b(), per_b()],
        out_specs=[tok(), tok()],
        out_shape=[jax.ShapeDtypeStruct((b, t, d), F32), jax.ShapeDtypeStruct((b, t, d), BF16)],
        compiler_params=_params(("parallel", "parallel")), name="resid_norm_mod",
    )(*args, gain.reshape(1, d), sc, sh)


def _norm_mod_router(x, gain, sc, sh, router_w_t, router_b, tm=512):
    b, t, d = x.shape
    tm = min(tm, t)
    e = router_w_t.shape[0]
    return pl.pallas_call(
        _norm_mod_router_kernel,
        grid=(b, t // tm),
        in_specs=[pl.BlockSpec((1, tm, d), lambda bi, i: (bi, i, 0)),
                  pl.BlockSpec((1, d), lambda bi, i: (0, 0)),
                  pl.BlockSpec((1, 1, d), lambda bi, i: (bi, 0, 0)),
                  pl.BlockSpec((1, 1, d), lambda bi, i: (bi, 0, 0)),
                  pl.BlockSpec((e, d), lambda bi, i: (0, 0)),
                  pl.BlockSpec((e, 1), lambda bi, i: (0, 0))],
        out_specs=[pl.BlockSpec((1, tm * ROW_TILES, LANES), lambda bi, i: (bi, i, 0)),
                   pl.BlockSpec((1, e, tm), lambda bi, i: (bi, 0, i))],
        out_shape=[jax.ShapeDtypeStruct((b, t * ROW_TILES, LANES), F32),
                   jax.ShapeDtypeStruct((b, e, t), F32)],
        compiler_params=_params(("parallel", "parallel")),
        name="norm_mod_router",
    )(x, gain.reshape(1, d), sc, sh, router_w_t, router_b.reshape(e, 1))


def _mm_kernel(*refs, n_terms, a_batched, w_batched, has_res):
    a_refs = refs[:n_terms]
    w_refs = refs[n_terms:2 * n_terms]
    o_ref = refs[-1]
    acc = None
    for a_ref, w_ref, ab, wb in zip(a_refs, w_refs, a_batched, w_batched):
        a = a_ref[0] if ab else a_ref[...]
        w = w_ref[0] if wb else w_ref[...]
        p = jnp.dot(a, w, preferred_element_type=F32)
        acc = p if acc is None else acc + p
    if has_res:
        res_ref, gate_ref = refs[2 * n_terms], refs[2 * n_terms + 1]
        acc = res_ref[0] + gate_ref[0] * acc
    o_ref[0] = acc.astype(o_ref.dtype)


def _matmul(a_list, w_list, out_dtype, *, batch, rows, tm, tn, res=None, gate=None, name="matmul"):
    n = w_list[0].shape[-1]
    tm = min(tm, rows)
    tn = min(tn, n)
    a_batched = tuple(a.ndim == 3 for a in a_list)
    w_batched = tuple(w.ndim == 3 for w in w_list)
    in_specs = []
    for a, ab in zip(a_list, a_batched):
        k = a.shape[-1]
        in_specs.append(pl.BlockSpec((1, tm, k), lambda b, i, j: (b, i, 0)) if ab
                        else pl.BlockSpec((tm, k), lambda b, i, j: (i, 0)))
    for w, wb in zip(w_list, w_batched):
        k = w.shape[-2]
        in_specs.append(pl.BlockSpec((1, k, tn), lambda b, i, j: (b, 0, j)) if wb
                        else pl.BlockSpec((k, tn), lambda b, i, j: (0, j)))
    args = list(a_list) + list(w_list)
    if res is not None:
        in_specs.append(pl.BlockSpec((1, tm, tn), lambda b, i, j: (b, i, j)))
        in_specs.append(pl.BlockSpec((1, 1, tn), lambda b, i, j: (b, 0, j)))
        args += [res, gate]
    kern = functools.partial(_mm_kernel, n_terms=len(a_list), a_batched=a_batched, w_batched=w_batched,
                             has_res=res is not None)
    return pl.pallas_call(
        kern,
        grid=(batch, rows // tm, n // tn),
        in_specs=in_specs,
        out_specs=pl.BlockSpec((1, tm, tn), lambda b, i, j: (b, i, j)),
        out_shape=jax.ShapeDtypeStruct((batch, rows, n), out_dtype),
        compiler_params=_params(("parallel", "parallel", "parallel")),
        name=name,
    )(*args)


def _pair_ones():
    r = lax.broadcasted_iota(jnp.int32, (LANES, LANES), 0) // RWKV_HEAD
    c = lax.broadcasted_iota(jnp.int32, (LANES, LANES), 1) // RWKV_HEAD
    return jnp.where(r == c, 1.0, 0.0).astype(BF16)


def _head_sum(x, ones_bf16):
    return _dot_exact_rhs(x, ones_bf16)


def _token_shift(ref, s, n, t, mu_p, mu_n):
    x = ref[0, s:s + n, :]
    w = x.shape[-1]
    ridx = lax.broadcasted_iota(jnp.int32, (n, w), 0)
    prev_row = ref[0, s - 1:s, :] if s > 0 else jnp.zeros((1, w), F32)
    next_row = ref[0, s + n:s + n + 1, :] if s + n < t else jnp.zeros((1, w), F32)
    prev = jnp.where(ridx == 0, prev_row, pltpu.roll(x, 1, 0))
    nxt = jnp.where(ridx == n - 1, next_row, pltpu.roll(x, n - 1, 0))
    return x + mu_p * (prev - x) + mu_n * (nxt - x)


def _rwkv_prep_kernel(pr_ref, pk_ref, pv_ref, pl_ref, par_ref, lpar_ref, w2_ref, a2_ref, g2_ref,
                      r_ref, v_ref, kk_ref, g_ref, ks_ref, lw_ref, kd_ref, alr_ref, *, t, rows):
    ones = _pair_ones()
    par = par_ref[...]
    lpar = lpar_ref[...]
    k_k, k_a = par[10:11], par[11:12]
    for s in range(0, t, rows):
        sl = slice(s, s + rows)
        r = _token_shift(pr_ref, s, rows, t, par[0:1], par[1:2])
        k = _token_shift(pk_ref, s, rows, t, par[2:3], par[3:4])
        v = _token_shift(pv_ref, s, rows, t, par[4:5], par[5:6])
        lo = _token_shift(pl_ref, s, rows, t, lpar[0:1], lpar[1:2])
        xwa, xg = lo[:, :LANES], lo[:, LANES:]
        r_ref[0, sl, :] = r
        v_ref[0, sl, :] = v
        g_ref[0, sl, :] = _bdot(_sigmoid(xg), g2_ref[...])
        kk = k * k_k
        kk = kk * lax.rsqrt(_head_sum(kk * kk, ones) + KK_EPS)
        kk_ref[0, sl, :] = kk
        tw = jnp.tanh(xwa)
        xa = xwa
        ksum = None
        for d in range(2):
            z = -(par[6 + d:7 + d] + _bdot(tw, w2_ref[d]))
            softplus = jnp.maximum(z, 0.0) + jnp.log(1.0 + jnp.exp(-jnp.abs(z)))
            w_log = -softplus - 0.5
            lw_ref[d, 0, sl, :] = -jnp.exp(w_log)
            a = _sigmoid(par[8 + d:9 + d] + _bdot(xa, a2_ref[d]))
            alr_ref[d, 0, sl, :] = a
            k_d = k * (1.0 + (a - 1.0) * k_a)
            kd_ref[d, 0, sl, :] = k_d
            ksum = k_d if ksum is None else ksum + k_d
        ks_ref[0, sl, :] = ksum


def _rwkv_prep(pr, pk, pv, plo, par, lpar, w2, a2, g2):
    b, t, _ = pr.shape
    npair = D_A // LANES
    rows = min(256, t)
    lora = plo.shape[-1]
    slab = lambda: pl.BlockSpec((1, t, LANES), lambda bi, p: (bi, 0, p))
    dslab = lambda: pl.BlockSpec((2, 1, t, LANES), lambda bi, p: (0, bi, 0, p))
    one = jax.ShapeDtypeStruct((b, t, D_A), F32)
    two = jax.ShapeDtypeStruct((2, b, t, D_A), F32)
    return pl.pallas_call(
        functools.partial(_rwkv_prep_kernel, t=t, rows=rows),
        grid=(b, npair),
        in_specs=[slab(), slab(), slab(),
                  pl.BlockSpec((1, t, lora), lambda bi, p: (bi, 0, 0)),
                  pl.BlockSpec((16, LANES), lambda bi, p: (0, p)),
                  pl.BlockSpec((8, lora), lambda bi, p: (0, 0)),
                  pl.BlockSpec((2, LANES, LANES), lambda bi, p: (0, 0, p)),
                  pl.BlockSpec((2, LANES, LANES), lambda bi, p: (0, 0, p)),
                  pl.BlockSpec((G_LORA, LANES), lambda bi, p: (0, p))],
        out_specs=[slab(), slab(), slab(), slab(), slab(), dslab(), dslab(), dslab()],
        out_shape=[one, one, one, one, one, two, two, two],
        compiler_params=_params(("parallel", "parallel")),
        name="rwkv_prep",
    )(pr, pk, pv, plo, par, lpar, w2, a2, g2)


_gram_dot = _bdot
_inv_dot = _bdot
_wu_dot = _bdot
_state_dot = _bdot


def _wkv_local(chunks, sgn, consts):
    n = WKV_CHUNK
    n2 = 2 * n
    tri, keep, dt, eye, lvl = consts
    order = sgn * dt
    strict = order > 0
    incl = order >= 0
    tri_b = jnp.where(sgn * tri >= 0, 1.0, 0.0).astype(BF16)
    each = lambda f, *ls: [f(*xs) for xs in zip(*ls)]

    def stack(x):
        return jnp.where(keep, jnp.concatenate([x, x], axis=0), 0.0)

    lws = [c[1] for c in chunks]
    cums = each(lambda lw: _dot_exact_lhs(tri_b, lw), lws)
    ctots = each(lambda lw: jnp.sum(lw, axis=0, keepdims=True), lws)
    lhs, rhs, kh2s, bh2s, v2s, at2s, rt2s = [], [], [], [], [], [], []
    for (r, lw, k, v, kk, alr), cum, ctot in zip(chunks, cums, ctots):
        e_neg = jnp.exp(-cum)
        e_rem = jnp.exp(ctot - cum)
        b = kk * alr
        rt2 = stack(r * jnp.exp(cum))
        at2 = stack(-kk * jnp.exp(cum - lw))
        lhs.append(jnp.concatenate([at2, rt2], axis=0))
        rhs.append(jnp.concatenate([stack(b * e_neg), stack(k * e_neg)], axis=0))
        kh2s.append(stack(k * e_rem))
        bh2s.append(stack(b * e_rem))
        v2s.append(stack(v))
        at2s.append(at2)
        rt2s.append(rt2)
    gs = each(lambda x, y: _gram_dot(x, y, NT_DIMS), lhs, rhs)
    a_abs = [jnp.where(strict, g[:n2, :n2], 0.0) for g in gs]
    a_rbs = [jnp.where(incl, g[n2:, :n2], 0.0) for g in gs]
    a_kks = [jnp.concatenate([jnp.where(strict, g[:n2, n2:], 0.0), jnp.where(incl, g[n2:, n2:], 0.0)], axis=0)
             for g in gs]
    m0 = sgn * lvl[0] == 1
    tinvs = [eye + jnp.where(m0, a, 0.0) for a in a_abs]
    for m in lvl[1:]:
        msk = sgn * m == 1
        cts = each(lambda a, t: _inv_dot(jnp.where(msk, a, 0.0), t), a_abs, tinvs)
        tinvs = each(lambda t, ct: t + _inv_dot(t, ct), tinvs, cts)
    kvs = each(_bdot, a_kks, v2s)
    wus = each(lambda t, at2, kv: _wu_dot(t, jnp.concatenate([at2, kv[:n2]], axis=1)), tinvs, at2s, kvs)
    rys = each(lambda rt2, kv, a_rb, wu: jnp.concatenate([rt2, kv[n2:]], axis=1) + _bdot(a_rb, wu),
               rt2s, kvs, a_rbs, wus)
    bws = each(lambda bh2, wu: _bdot(bh2, wu, TN_DIMS), bh2s, wus)
    khv = each(lambda kh2, v2: _bdot(kh2, v2, TN_DIMS), kh2s, v2s)
    out = []
    for ry, bw, kv, ctot in zip(rys, bws, khv, ctots):
        decay_col = jnp.sum(jnp.where(eye > 0, jnp.exp(ctot), 0.0), axis=1, keepdims=True)
        out.append((ry[:, :LANES], ry[:, LANES:], bw[:, :LANES], bw[:, LANES:] + kv, decay_col))
    return out


def _wkv_consts():
    n = WKV_CHUNK
    n2 = 2 * n
    tri = lax.broadcasted_iota(jnp.int32, (n, n), 0) - lax.broadcasted_iota(jnp.int32, (n, n), 1)
    lane = lax.broadcasted_iota(jnp.int32, (n2, LANES), 1)
    rr = lax.broadcasted_iota(jnp.int32, (n2, LANES), 0)
    keep = (lane >= RWKV_HEAD) == (rr >= n)
    ri = lax.broadcasted_iota(jnp.int32, (n2, n2), 0)
    ci = lax.broadcasted_iota(jnp.int32, (n2, n2), 1)
    r2 = ri & (n - 1)
    c2 = ci & (n - 1)
    dt = r2 - c2
    eye = jnp.where(ri == ci, 1.0, 0.0).astype(F32)
    lvl = []
    k = 0
    while (1 << k) < n:
        same = (r2 >> (k + 1)) == (c2 >> (k + 1))
        diff = ((r2 >> k) & 1) - ((c2 >> k) & 1)
        lvl.append(jnp.where(same, diff, 0))
        k += 1
    return tri, keep, dt, eye, lvl


def _wkv_kernel(r_ref, v_ref, kk_ref, lw_ref, kd_ref, alr_ref, s0_ref, y_ref, sfin_ref, st_ref, *, n_sub):
    d = pl.program_id(0)
    c = pl.program_id(3)
    nc = pl.num_programs(3)

    @pl.when(c == 0)
    def _():
        st_ref[...] = s0_ref[0, 0, 0]

    sgn = 1 - 2 * d
    rows = []
    for u in range(n_sub):
        ui = u + d * (n_sub - 1 - 2 * u)
        rows.append(pl.ds(pl.multiple_of(ui * WKV_CHUNK, WKV_CHUNK), WKV_CHUNK))
    chunks = [(r_ref[0, rw, :], lw_ref[0, 0, rw, :], kd_ref[0, 0, rw, :], v_ref[0, rw, :], kk_ref[0, rw, :],
               alr_ref[0, 0, rw, :]) for rw in rows]
    local = _wkv_local(chunks, sgn, _wkv_consts())
    st = st_ref[...]
    for rw, (rhat, yhat, mmat, nmat, decay_col) in zip(rows, local):
        y2 = _state_dot(rhat, st) + yhat
        y_ref[0, 0, rw, :] = y2[:WKV_CHUNK] + y2[WKV_CHUNK:]
        st = decay_col * st + _state_dot(mmat, st) + nmat
    st_ref[...] = st

    @pl.when(c == nc - 1)
    def _():
        sfin_ref[0, 0, 0] = st


def _wkv(r, v, kk, lw, kd, alr, s0):
    b, t, _ = r.shape
    npair = D_A // LANES
    n_sub = min(WKV_SUB, t // WKV_CHUNK)
    blk = n_sub * WKV_CHUNK
    nc = t // blk

    def tmap(d, bi, p, c):
        return (bi, c + d * (nc - 1 - 2 * c), p)

    def dmap(d, bi, p, c):
        return (d, bi, c + d * (nc - 1 - 2 * c), p)

    smap = lambda d, bi, p, c: (d, bi, p, 0, 0)
    shared = lambda: pl.BlockSpec((1, blk, LANES), tmap)
    direc = lambda: pl.BlockSpec((1, 1, blk, LANES), dmap)
    state = lambda: pl.BlockSpec((1, 1, 1, LANES, LANES), smap)
    return pl.pallas_call(
        functools.partial(_wkv_kernel, n_sub=n_sub),
        grid=(2, b, npair, nc),
        in_specs=[shared(), shared(), shared(), direc(), direc(), direc(), state()],
        out_specs=[direc(), state()],
        out_shape=[jax.ShapeDtypeStruct((2, b, t, D_A), F32),
                   jax.ShapeDtypeStruct((2, b, npair, LANES, LANES), F32)],
        scratch_shapes=[pltpu.VMEM((LANES, LANES), F32)],
        compiler_params=_params(("parallel", "parallel", "parallel", "arbitrary")),
        name="wkv",
    )(r, v, kk, lw, kd, alr, s0)


def _rwkv_finish_kernel(y_ref, r_ref, v_ref, ks_ref, g_ref, par_ref, o_ref):
    ones = _pair_ones()
    par = par_ref[...]
    inv_n = 1.0 / RWKV_HEAD
    for p in range(D_A // LANES):
        sl = slice(p * LANES, (p + 1) * LANES)
        y = y_ref[0, 0, :, sl] + y_ref[1, 0, :, sl]
        mu = _head_sum(y, ones) * inv_n
        yc = y - mu
        var = _head_sum(yc * yc, ones) * inv_n
        yn = yc * lax.rsqrt(var + LNX_EPS) * par[0:1, sl] + par[1:2, sl]
        bonus = _head_sum(r_ref[0, :, sl] * ks_ref[0, :, sl] * par[2:3, sl], ones) * v_ref[0, :, sl]
        o_ref[0, :, sl] = ((yn + bonus) * g_ref[0, :, sl]).astype(BF16)


def _rwkv_finish(y, r, v, ks, g, par, tm=256):
    b, t, d = r.shape
    tile = lambda: pl.BlockSpec((1, tm, d), lambda bi, i: (bi, i, 0))
    return pl.pallas_call(
        _rwkv_finish_kernel,
        grid=(b, t // tm),
        in_specs=[pl.BlockSpec((2, 1, tm, d), lambda bi, i: (0, bi, i, 0)), tile(), tile(), tile(), tile(),
                  pl.BlockSpec((8, d), lambda bi, i: (0, 0))],
        out_specs=tile(),
        out_shape=jax.ShapeDtypeStruct((b, t, d), BF16),
        compiler_params=_params(("parallel", "parallel")),
        name="rwkv_finish",
    )(y, r, v, ks, g, par)


def _head_rms(x, gain):
    return x * lax.rsqrt(jnp.mean(x * x, axis=-1, keepdims=True) + NORM_EPS) * gain


def _rope(x, cos_e, sin_s):
    lane = lax.broadcasted_iota(jnp.int32, x.shape, 1)
    swapped = jnp.where((lane & 1) == 0, pltpu.roll(x, LANES - 1, 1), pltpu.roll(x, 1, 1))
    return x * cos_e + swapped * sin_s


def _attn_prep_kernel(*refs, with_q):
    if with_q:
        pq_ref, pkv_ref, gains_ref, cos_ref, sin_ref, q_ref, k_ref, v_ref = refs
    else:
        pkv_ref, gains_ref, k_ref, v_ref = refs
    gains = gains_ref[...]
    kv_w = H_KV * HEAD_DIM
    for h in range(H_KV):
        sl = slice(h * HEAD_DIM, (h + 1) * HEAD_DIM)
        kh = _head_rms(pkv_ref[0, :, sl], gains[1:2])
        if with_q:
            kh = _rope(kh, cos_ref[...], sin_ref[...])
        k_ref[0, :, sl] = kh.astype(BF16)
    v_ref[0] = pkv_ref[0, :, kv_w:].astype(BF16)
    if with_q:
        scale = HEAD_DIM ** -0.5 * float(np.log2(np.e))
        for h in range(H_Q):
            sl = slice(h * HEAD_DIM, (h + 1) * HEAD_DIM)
            qh = _rope(_head_rms(pq_ref[0, :, sl], gains[0:1]), cos_ref[...], sin_ref[...])
            q_ref[0, :, sl] = (qh * scale).astype(BF16)


def _attn_prep(pq, pkv, gains, cos_e, sin_s, tm=256):
    b, t, _ = pkv.shape
    with_q = pq is not None
    kv_w = H_KV * HEAD_DIM
    tm = min(tm, t)
    kv_spec = lambda: pl.BlockSpec((1, tm, kv_w), lambda bi, i: (bi, i, 0))
    in_specs = [pl.BlockSpec((1, tm, 2 * kv_w), lambda bi, i: (bi, i, 0)),
                pl.BlockSpec((8, HEAD_DIM), lambda bi, i: (0, 0))]
    args = [pkv, gains]
    out_specs = [kv_spec(), kv_spec()]
    out_shape = [jax.ShapeDtypeStruct((b, t, kv_w), BF16)] * 2
    if with_q:
        in_specs = [pl.BlockSpec((1, tm, D_B), lambda bi, i: (bi, i, 0))] + in_specs + [
            pl.BlockSpec((tm, HEAD_DIM), lambda bi, i: (i, 0)),
            pl.BlockSpec((tm, HEAD_DIM), lambda bi, i: (i, 0))]
        args = [pq] + args + [cos_e, sin_s]
        out_specs = [pl.BlockSpec((1, tm, D_B), lambda bi, i: (bi, i, 0))] + out_specs
        out_shape = [jax.ShapeDtypeStruct((b, t, D_B), BF16)] + out_shape
    return pl.pallas_call(
        functools.partial(_attn_prep_kernel, with_q=with_q),
        grid=(b, t // tm),
        in_specs=in_specs,
        out_specs=out_specs,
        out_shape=out_shape,
        compiler_params=_params(("parallel", "parallel")),
        name="attn_prep",
    )(*args)


def _attn_kernel(q_ref, k_ref, v_ref, o_ref):
    k = k_ref[0]
    v = v_ref[0]
    for g in range(Q_PER_KV):
        sl = slice(g * HEAD_DIM, (g + 1) * HEAD_DIM)
        s = lax.dot_general(q_ref[0, :, sl], k, NT_DIMS, preferred_element_type=F32)
        p = jnp.exp2(s - jnp.max(s, axis=-1, keepdims=True))
        o = jnp.dot(p.astype(BF16), v, preferred_element_type=F32)
        o_ref[0, :, sl] = (o[:, :HEAD_DIM] / o[:, HEAD_DIM:HEAD_DIM + 1]).astype(BF16)


def _attention(q, k_all, v_all, tq=256):
    b, t, _ = q.shape
    s_len = k_all.shape[1]
    gw = Q_PER_KV * HEAD_DIM
    return pl.pallas_call(
        _attn_kernel,
        grid=(b, H_KV, t // tq),
        in_specs=[pl.BlockSpec((1, tq, gw), lambda bi, h, i: (bi, i, h)),
                  pl.BlockSpec((1, s_len, HEAD_DIM), lambda bi, h, i: (bi, 0, h)),
                  pl.BlockSpec((1, s_len, 2 * HEAD_DIM), lambda bi, h, i: (bi, 0, h))],
        out_specs=pl.BlockSpec((1, tq, gw), lambda bi, h, i: (bi, i, h)),
        out_shape=jax.ShapeDtypeStruct((b, t, H_Q * HEAD_DIM), BF16),
        compiler_params=_params(("parallel", "parallel", "parallel")),
        name="attention",
    )(q, k_all, v_all)


def _dft_channels_kernel(h_ref, w_ref, c_ref, s_ref):
    p = jnp.dot(h_ref[0], w_ref[...], preferred_element_type=F32)
    c_ref[0] = p[:, :FOURIER_GROUP_DIM].astype(BF16)
    s_ref[0] = p[:, FOURIER_GROUP_DIM:].astype(BF16)


def _dft_channels(h, w_cs, tm=512):
    b, t, d = h.shape
    tm = min(tm, t)
    gd = FOURIER_GROUP_DIM
    blk = lambda: pl.BlockSpec((1, tm, gd), lambda bi, i, g: (bi, i, g))
    return pl.pallas_call(
        _dft_channels_kernel,
        grid=(b, t // tm, d // gd),
        in_specs=[blk(), pl.BlockSpec((gd, 2 * gd), lambda bi, i, g: (0, 0))],
        out_specs=[blk(), blk()],
        out_shape=[jax.ShapeDtypeStruct((b, t, d), BF16)] * 2,
        compiler_params=_params(("parallel", "parallel", "parallel")),
        name="dft_channels",
    )(h, w_cs)


def _dft_mats(n):
    i = lax.broadcasted_iota(jnp.int32, (n, n), 0)
    j = lax.broadcasted_iota(jnp.int32, (n, n), 1)
    ang = ((i * j) % n).astype(F32) * (2.0 * np.pi / n)
    scale = n ** -0.5
    return jnp.cos(ang) * scale, jnp.sin(ang) * scale


MOE_TILE = 512


def _moe_kernel(src_ref, grp_ref, live_ref, h_hbm, gs_ref, wg_ref, wu_ref, wd_ref, o_hbm,
                rows_ref, h16_ref, acc_ref, gsem, ssem, *, tm, n_tok):
    i = pl.program_id(0)
    j = pl.program_id(1)
    n_live = live_ref[0]
    live = i < n_live
    slot = i % 2

    def row(ref, t):
        return ref.at[pl.ds(pl.multiple_of(t * ROW_TILES, ROW_TILES), ROW_TILES)]

    def start_gather(tile, sl):
        def issue(k, c):
            tok = jnp.maximum(src_ref[tile * tm + k], 0)
            pltpu.make_async_copy(row(h_hbm, tok), row(rows_ref.at[sl], k), gsem.at[sl]).start()
            return c
        lax.fori_loop(0, tm, issue, 0, unroll=8)

    def wait_gather(sl):
        pltpu.make_async_copy(h_hbm.at[pl.ds(0, tm * ROW_TILES)], rows_ref.at[sl], gsem.at[sl]).wait()

    def start_scatter(tile, sl):
        def issue(k, c):
            tok = src_ref[tile * tm + k]
            tok = jnp.where(tok < 0, n_tok + k, tok)
            pltpu.make_async_copy(row(rows_ref.at[sl], k), row(o_hbm, tok), ssem.at[sl]).start()
            return c
        lax.fori_loop(0, tm, issue, 0, unroll=8)

    def wait_scatter(sl):
        pltpu.make_async_copy(rows_ref.at[sl], o_hbm.at[pl.ds(0, tm * ROW_TILES)], ssem.at[sl]).wait()

    @pl.when(jnp.logical_and(live, j == 0))
    def _():
        @pl.when(i == 0)
        def _():
            start_gather(0, 0)
            rows_ref[1] = jnp.zeros(rows_ref.shape[1:], F32)
            spare = pltpu.make_async_copy(rows_ref.at[1], o_hbm.at[pl.ds(n_tok * ROW_TILES, tm * ROW_TILES)],
                                          ssem.at[1])
            spare.start()
            spare.wait()
        wait_gather(slot)
        h16_ref[...] = _from_token_rows(rows_ref, tm, lead=(slot,)).astype(BF16)
        acc_ref[...] = jnp.zeros_like(acc_ref)

    @pl.when(jnp.logical_and(live, j == 1))
    def _():
        @pl.when(i > 0)
        def _():
            wait_scatter(1 - slot)

        @pl.when(i + 1 < n_live)
        def _():
            start_gather(i + 1, 1 - slot)

    @pl.when(live)
    def _():
        h = h16_ref[...]
        u = _silu(_bdot(h, wg_ref[0, 0])) * _bdot(h, wu_ref[0, 0])
        lane = lax.broadcasted_iota(jnp.int32, gs_ref.shape, 1)
        gcol = jnp.sum(jnp.where(lane == j, gs_ref[...], 0.0), axis=1, keepdims=True)
        acc_ref[...] += gcol * _bdot(u, wd_ref[0, 0])

    @pl.when(jnp.logical_and(live, j == pl.num_programs(1) - 1))
    def _():
        _to_token_rows(rows_ref, acc_ref[...], lead=(slot,))
        start_scatter(i, slot)

        @pl.when(i == n_live - 1)
        def _():
            wait_scatter(slot)


def _route_meta(gates, tm):
    n = gates.shape[0]
    ng, epg = N_EXPERT_GROUPS, EXPERTS_PER_GROUP
    gg = gates.reshape(n, ng, epg)
    grp = jnp.argmax(jnp.max(gg, axis=-1) > 0, axis=-1).astype(jnp.int32)
    onehot = (grp[:, None] == jnp.arange(ng)[None, :]).astype(jnp.int32)
    counts = jnp.sum(onehot, axis=0)
    padded = ((counts + tm - 1) // tm) * tm
    ends = jnp.cumsum(padded)
    starts = ends - padded
    rank = jnp.sum((jnp.cumsum(onehot, axis=0) - onehot) * onehot, axis=1)
    pos = jnp.sum(starts[None, :] * onehot, axis=1) + rank
    p_total = n + ng * tm
    src = jnp.full((p_total,), -1, jnp.int32).at[pos].set(jnp.arange(n, dtype=jnp.int32))
    gsel = jnp.sum(gg * onehot[:, :, None].astype(F32), axis=1)
    gs_sorted = jnp.zeros((p_total, epg), F32).at[pos].set(gsel)
    n_live = ends[-1] // tm
    tile_start = jnp.minimum(jnp.arange(p_total // tm, dtype=jnp.int32), n_live - 1) * tm
    tile_grp = jnp.minimum(jnp.sum(tile_start[:, None] >= ends[None, :], axis=1), ng - 1).astype(jnp.int32)
    return src, gs_sorted, tile_grp, n_live.astype(jnp.int32).reshape(1)


def _moe(h_rows, gates, wg, wu, wd, layer, tm=MOE_TILE):
    n = gates.shape[0]
    d = D_MODEL
    _, ne, _, de = wg.shape
    epg = EXPERTS_PER_GROUP
    src, gs_sorted, tile_grp, n_live = _route_meta(gates, tm)
    n_tiles = src.shape[0] // tm

    def wmap(i, j, src, grp, live):
        return (layer, grp[i] * epg + jnp.where(i < live[0], j, epg - 1), 0, 0)

    grid_spec = pltpu.PrefetchScalarGridSpec(
        num_scalar_prefetch=3,
        grid=(n_tiles, epg),
        in_specs=[pl.BlockSpec(memory_space=pl.ANY),
                  pl.BlockSpec((tm, epg), lambda i, j, src, grp, live: (i, 0)),
                  pl.BlockSpec((1, 1, d, de), wmap),
                  pl.BlockSpec((1, 1, d, de), wmap),
                  pl.BlockSpec((1, 1, de, d), wmap)],
        out_specs=pl.BlockSpec(memory_space=pl.ANY),
        scratch_shapes=[pltpu.VMEM((2, tm * ROW_TILES, LANES), F32), pltpu.VMEM((tm, d), BF16),
                        pltpu.VMEM((tm, d), F32), pltpu.SemaphoreType.DMA((2,)), pltpu.SemaphoreType.DMA((2,))],
    )
    return pl.pallas_call(
        functools.partial(_moe_kernel, tm=tm, n_tok=n),
        grid_spec=grid_spec,
        out_shape=jax.ShapeDtypeStruct(((n + tm) * ROW_TILES, LANES), F32),
        compiler_params=pltpu.CompilerParams(dimension_semantics=("arbitrary", "arbitrary"),
                                             vmem_limit_bytes=VMEM_LIMIT, disable_bounds_checks=True),
        name="moe",
    )(src, tile_grp, n_live, h_rows, gs_sorted, wg, wu, wd)


def _rope_tables(n):
    rows = n // GRID_W
    row = jnp.repeat(jnp.arange(rows), GRID_W).astype(F32)
    col = jnp.tile(jnp.arange(GRID_W), rows).astype(F32)
    inv = ROPE_THETA ** (-jnp.arange(0, ROPE_AXIS_DIM, 2, dtype=F32) / ROPE_AXIS_DIM)
    ang = jnp.concatenate([row[:, None] * inv, col[:, None] * inv], axis=-1)
    cos_e = jnp.repeat(jnp.cos(ang), 2, axis=-1)
    sin_s = jnp.stack([-jnp.sin(ang), jnp.sin(ang)], axis=-1).reshape(n, HEAD_DIM)
    return cos_e, sin_s


def _pad_rows(rows, n):
    a = jnp.stack(rows, axis=0)
    return jnp.pad(a, ((0, n - a.shape[0]), (0, 0)))


def _enter_layer(x, pending, gain, sc, sh):
    if pending is None:
        return x, _norm_mod(x, gain, sc, sh)
    f_rows, gt = pending
    return _resid(x, f_rows, gt, norm=(gain, sc, sh))


def _even_mixer(x, pending, ctx, mods, gain, w_in, mu_prev, mu_next, w0, w2, a0, a2, g2, k_k, k_a, r_k, ln_g, ln_b,
                q_gain, k_gain, w_out):
    bsz, seq, _ = x.shape
    sh1, sc1, gt1, csh1, csc1 = mods
    x, h_lat = _enter_layer(x, pending, gain, sc1, sh1)
    h_ctx = _norm_mod(ctx, gain, csc1, csh1)

    a_cols = 3 * D_A + W_LORA + A_LORA + G_LORA
    wb = w_in.astype(BF16)
    w_r, w_k, w_v = wb[:, :D_A], wb[:, D_A:2 * D_A], wb[:, 2 * D_A:3 * D_A]
    w_lo = wb[:, 3 * D_A:a_cols]
    w_q = wb[:, a_cols:a_cols + D_B]
    w_kv = wb[:, a_cols + D_B:]

    def proj(h, w, name):
        rows = h.shape[1]
        return _matmul([h], [w], F32, batch=bsz, rows=rows, tm=1024, tn=512, name=name)

    par = _pad_rows([mu_prev[:D_A], mu_next[:D_A], mu_prev[D_A:2 * D_A], mu_next[D_A:2 * D_A],
                     mu_prev[2 * D_A:3 * D_A], mu_next[2 * D_A:3 * D_A],
                     w0[0], w0[1], a0[0], a0[1], k_k, k_a], 16)
    lpar = _pad_rows([mu_prev[3 * D_A:], mu_next[3 * D_A:]], 8)
    fin_par = _pad_rows([ln_g, ln_b, r_k.reshape(-1)], 8)
    w2 = jnp.pad(w2, ((0, 0), (0, A_LORA), (0, 0)))
    a2 = jnp.pad(a2, ((0, 0), (W_LORA, 0), (0, 0)))

    def streams(h):
        return _rwkv_prep(proj(h, w_r, "proj_r"), proj(h, w_k, "proj_k"), proj(h, w_v, "proj_v"),
                          proj(h, w_lo, "proj_lora"), par, lpar, w2, a2, g2)

    rc, vc, kkc, _, _, lwc, kdc, alrc = streams(h_ctx)
    rl, vl, kkl, gl, ksl, lwl, kdl, alrl = streams(h_lat)
    s0 = jnp.zeros((2, bsz, D_A // LANES, LANES, LANES), F32)
    _, s_ctx = _wkv(rc, vc, kkc, lwc, kdc, alrc, s0)
    y, _ = _wkv(rl, vl, kkl, lwl, kdl, alrl, s_ctx)
    ya = _rwkv_finish(y, rl, vl, ksl, gl, fin_par)

    cos_e, sin_s = _rope_tables(seq)
    gains = _pad_rows([q_gain, k_gain], 8)
    k_c, v_c = _attn_prep(None, proj(h_ctx, w_kv, "proj_kv"), gains, None, None)
    q_l, k_l, v_l = _attn_prep(proj(h_lat, w_q, "proj_q"), proj(h_lat, w_kv, "proj_kv"), gains, cos_e, sin_s)
    v_all = jnp.concatenate([v_c, v_l], axis=1).reshape(bsz, -1, H_KV, HEAD_DIM)
    v_ext = jnp.concatenate([v_all, jnp.ones_like(v_all)], axis=-1).reshape(bsz, -1, 2 * H_KV * HEAD_DIM)
    yb = _attention(q_l, jnp.concatenate([k_c, k_l], axis=1), v_ext)

    wo = w_out.astype(BF16)
    return _matmul([ya, yb], [wo[:D_A], wo[D_A:]], F32, batch=bsz, rows=seq, tm=1024, tn=512,
                   res=x, gate=gt1, name="mixer_out")


def _fourier_mixer(x, pending, mods, gain, w_out):
    bsz, seq, d = x.shape
    sh1, sc1, gt1 = mods
    x, h = _enter_layer(x, pending, gain, sc1, sh1)
    cc, sc = _dft_mats(FOURIER_GROUP_DIM)
    ct, st = _dft_mats(seq)
    pc, ps = _dft_channels(h, jnp.concatenate([cc, sc], axis=1).astype(BF16))
    f = _matmul([ct.astype(BF16), (-st).astype(BF16)], [pc, ps], BF16, batch=bsz, rows=seq, tm=512, tn=512,
                name="dft_tokens")
    return _matmul([f], [w_out.astype(BF16)], F32, batch=bsz, rows=seq, tm=1024, tn=512, res=x, gate=gt1,
                   name="mixer_out")


def _moe_layer(x, mods, gain, router_w_t, router_b, w_gate, w_up, w_down, layer):
    bsz, seq, d = x.shape
    sh2, sc2, _ = mods
    h_rows, gates_t = _norm_mod_router(x, gain, sc2, sh2, router_w_t, router_b)
    gates = jnp.swapaxes(gates_t, 1, 2).reshape(bsz * seq, -1)
    return _moe(h_rows.reshape(bsz * seq * ROW_TILES, LANES), gates, w_gate, w_up, w_down, layer)


def kernel(x, c, ctx, c_ctx, mod_w, mod_b, norm_mix, norm_ffn, ab_w_in, ab_mu_prev, ab_mu_next,
           rwkv_w0, rwkv_w2, rwkv_a0, rwkv_a2, rwkv_g2, rwkv_k_k, rwkv_k_a, rwkv_r_k, rwkv_ln_g, rwkv_ln_b,
           attn_q_norm, attn_k_norm, ab_w_out, fourier_w_out, router_w, router_b,
           exp_w_gate, exp_w_up, exp_w_down):
    bsz, seq, d = x.shape
    depth = mod_w.shape[0]
    cvec = jnp.concatenate([c, c_ctx[None], jnp.zeros((8 - bsz - 1, d), F32)], axis=0)
    mod = _modulation(cvec, mod_w, mod_b)
    router_w_t = router_w.T
    pending = None
    for l in range(depth):
        m_lat = mod[l, :bsz, None, :]
        sh1, sc1, gt1, sh2, sc2, gt2 = jnp.split(m_lat, 6, axis=-1)
        if l % 2 == 0:
            e = l // 2
            m_ctx = mod[l, bsz:bsz + 1, None, :]
            csh1, csc1 = m_ctx[..., :d], m_ctx[..., d:2 * d]
            x = _even_mixer(x, pending, ctx, (sh1, sc1, gt1, csh1, csc1), norm_mix[l], ab_w_in[e], ab_mu_prev[e],
                            ab_mu_next[e], rwkv_w0[e], rwkv_w2[e], rwkv_a0[e], rwkv_a2[e], rwkv_g2[e],
                            rwkv_k_k[e], rwkv_k_a[e], rwkv_r_k[e], rwkv_ln_g[e], rwkv_ln_b[e],
                            attn_q_norm[e], attn_k_norm[e], ab_w_out[e])
        else:
            x = _fourier_mixer(x, pending, (sh1, sc1, gt1), norm_mix[l], fourier_w_out[l // 2])
        f_rows = _moe_layer(x, (sh2, sc2, gt2), norm_ffn[l], router_w_t, router_b,
                            exp_w_gate, exp_w_up, exp_w_down, l)
        pending = (f_rows, gt2)
    return _resid(x, pending[0], pending[1])
```

```python
import functools

import jax
import jax.numpy as jnp
import numpy as np
from jax import lax
from jax.experimental import pallas as pl
from jax.experimental.pallas import tpu as pltpu

F32 = jnp.float32
BF16 = jnp.bfloat16

D_MODEL = 2048
D_A = 1024
RWKV_HEAD = 64
W_LORA = 64
A_LORA = 64
G_LORA = 128
LNX_EPS = 64e-5
D_B = 1024
HEAD_DIM = 128
H_Q = 8
H_KV = 2
Q_PER_KV = H_Q // H_KV
GRID_W = 64
ROPE_THETA = 10000.0
ROPE_AXIS_DIM = HEAD_DIM // 2
FOURIER_GROUPS = 8
FOURIER_GROUP_DIM = D_MODEL // FOURIER_GROUPS
N_EXPERTS = 16
N_EXPERT_GROUPS = 4
EXPERTS_PER_GROUP = 4
D_EXPERT = 512
NORM_EPS = 1e-6
KK_EPS = 1e-12

LANES = 128
WKV_CHUNK = 64
WKV_SUB = 16
VMEM_LIMIT = 56 * 1024 * 1024

NT_DIMS = (((1,), (1,)), ((), ()))
TN_DIMS = (((0,), (0,)), ((), ()))


def _params(sem):
    return pltpu.CompilerParams(dimension_semantics=sem, vmem_limit_bytes=VMEM_LIMIT)


def _bdot(a, b, dims=None):
    a = a.astype(BF16)
    b = b.astype(BF16)
    if dims is None:
        return jnp.dot(a, b, preferred_element_type=F32)
    return lax.dot_general(a, b, dims, preferred_element_type=F32)


def _split2(x):
    hi = x.astype(BF16)
    lo = (x - hi.astype(F32)).astype(BF16)
    return hi, lo


def _split3(x):
    hi = x.astype(BF16)
    r1 = x - hi.astype(F32)
    mid = r1.astype(BF16)
    lo = (r1 - mid.astype(F32)).astype(BF16)
    return hi, mid, lo


def _dot3(a, b, dims=None):
    ah, al = _split2(a)
    bh, bl = _split2(b)
    return _bdot(ah, bh, dims) + (_bdot(ah, bl, dims) + _bdot(al, bh, dims))


def _dot_exact_lhs(a_bf16, x, dims=None):
    h, m, l = _split3(x)
    return _bdot(a_bf16, h, dims) + (_bdot(a_bf16, m, dims) + _bdot(a_bf16, l, dims))


def _dot_exact_rhs(x, b_bf16):
    h, l = _split2(x)
    return _bdot(h, b_bf16) + _bdot(l, b_bf16)


def _sigmoid(x):
    return 1.0 / (1.0 + jnp.exp(-x))


def _silu(x):
    return x * _sigmoid(x)


def _mod_kernel(c_ref, w_ref, b_ref, o_ref):
    s = _silu(c_ref[...])
    o_ref[0] = _bdot(s, w_ref[0]) + b_ref[0]


def _modulation(cvec, mod_w, mod_b):
    depth, d, n = mod_w.shape
    tn = 1024
    return pl.pallas_call(
        _mod_kernel,
        grid=(depth, n // tn),
        in_specs=[pl.BlockSpec((8, d), lambda l, j: (0, 0)),
                  pl.BlockSpec((1, d, tn), lambda l, j: (l, 0, j)),
                  pl.BlockSpec((1, 1, tn), lambda l, j: (l, 0, j))],
        out_specs=pl.BlockSpec((1, 8, tn), lambda l, j: (l, 0, j)),
        out_shape=jax.ShapeDtypeStruct((depth, 8, n), F32),
        compiler_params=_params(("parallel", "parallel")),
        name="modulation",
    )(cvec, mod_w, mod_b.reshape(depth, 1, n))


def _normed(x, gain, sc, sh):
    y = x * lax.rsqrt(jnp.mean(x * x, axis=-1, keepdims=True) + NORM_EPS)
    return (y * gain) * (1.0 + sc) + sh


def _norm_mod_kernel(x_ref, g_ref, sc_ref, sh_ref, o_ref):
    o_ref[0] = _normed(x_ref[0], g_ref[...], sc_ref[0], sh_ref[0]).astype(BF16)


def _top2_gates(logits, bias):
    s = _sigmoid(logits)
    sel = s + bias
    neg = jnp.float32(-jnp.inf)
    flags, scores = [], []
    for g in range(N_EXPERT_GROUPS):
        xs = [sel[g * EXPERTS_PER_GROUP + i:g * EXPERTS_PER_GROUP + i + 1] for i in range(EXPERTS_PER_GROUP)]
        m1 = functools.reduce(jnp.maximum, xs)
        first, taken = [], None
        for x in xs:
            f = (x == m1) if taken is None else jnp.logical_and(x == m1, jnp.logical_not(taken))
            taken = f if taken is None else jnp.logical_or(taken, f)
            first.append(f)
        rest = [jnp.where(f, neg, x) for f, x in zip(first, xs)]
        m2 = functools.reduce(jnp.maximum, rest)
        second, taken = [], None
        for f, x in zip(first, xs):
            c = jnp.logical_and(x == m2, jnp.logical_not(f))
            if taken is not None:
                c = jnp.logical_and(c, jnp.logical_not(taken))
            taken = c if taken is None else jnp.logical_or(taken, c)
            second.append(c)
        flags.append([jnp.logical_or(f, c) for f, c in zip(first, second)])
        scores.append(m1 + m2)
    best = functools.reduce(jnp.maximum, scores)
    taken, rows = None, []
    for g in range(N_EXPERT_GROUPS):
        bg = (scores[g] == best) if taken is None else jnp.logical_and(scores[g] == best, jnp.logical_not(taken))
        taken = bg if taken is None else jnp.logical_or(taken, bg)
        for i in range(EXPERTS_PER_GROUP):
            e = g * EXPERTS_PER_GROUP + i
            rows.append(jnp.where(jnp.logical_and(bg, flags[g][i]), s[e:e + 1], 0.0))
    denom = functools.reduce(jnp.add, rows)
    inv = 1.0 / denom
    return [r * inv for r in rows]


ROW_TILES = D_MODEL // LANES


def _to_token_rows(ref, x, lead=()):
    n = x.shape[0]
    for s in range(ROW_TILES):
        ref[lead + (pl.ds(s, n, stride=ROW_TILES), slice(None))] = x[:, s * LANES:(s + 1) * LANES]


def _from_token_rows(ref, n, lead=()):
    return jnp.concatenate([ref[lead + (pl.ds(s, n, stride=ROW_TILES), slice(None))] for s in range(ROW_TILES)],
                           axis=-1)


def _norm_mod_router_kernel(x_ref, g_ref, sc_ref, sh_ref, rw_ref, rb_ref, o_ref, gates_ref):
    h = _normed(x_ref[0], g_ref[...], sc_ref[0], sh_ref[0])
    _to_token_rows(o_ref, h, lead=(0,))
    logits = _dot3(rw_ref[...], h, NT_DIMS)
    for e, row in enumerate(_top2_gates(logits, rb_ref[...])):
        gates_ref[0, e:e + 1, :] = row


def _norm_mod(x, gain, sc, sh, tm=512):
    b, t, d = x.shape
    tm = min(tm, t)
    per_b = sc.shape[0] > 1
    mod_map = (lambda bi, i: (bi, 0, 0)) if per_b else (lambda bi, i: (0, 0, 0))
    return pl.pallas_call(
        _norm_mod_kernel,
        grid=(b, t // tm),
        in_specs=[pl.BlockSpec((1, tm, d), lambda bi, i: (bi, i, 0)),
                  pl.BlockSpec((1, d), lambda bi, i: (0, 0)),
                  pl.BlockSpec((1, 1, d), mod_map),
                  pl.BlockSpec((1, 1, d), mod_map)],
        out_specs=pl.BlockSpec((1, tm, d), lambda bi, i: (bi, i, 0)),
        out_shape=jax.ShapeDtypeStruct((b, t, d), BF16),
        compiler_params=_params(("parallel", "parallel")),
        name="norm_mod",
    )(x, gain.reshape(1, d), sc, sh)


def _resid_kernel(x_ref, f_ref, gt_ref, xo_ref):
    xo_ref[0] = x_ref[0] + gt_ref[0] * _from_token_rows(f_ref, x_ref.shape[1])


def _resid_norm_mod_kernel(x_ref, f_ref, gt_ref, g_ref, sc_ref, sh_ref, xo_ref, o_ref):
    x = x_ref[0] + gt_ref[0] * _from_token_rows(f_ref, x_ref.shape[1])
    xo_ref[0] = x
    o_ref[0] = _normed(x, g_ref[...], sc_ref[0], sh_ref[0]).astype(BF16)


def _resid(x, f_rows, gt, norm=None, tm=512):
    b, t, d = x.shape
    tm = min(tm, t)
    nt = t // tm
    tok = lambda: pl.BlockSpec((1, tm, d), lambda bi, i: (bi, i, 0))
    per_b = lambda: pl.BlockSpec((1, 1, d), lambda bi, i: (bi, 0, 0))
    in_specs = [tok(), pl.BlockSpec((tm * ROW_TILES, LANES), lambda bi, i: (bi * nt + i, 0)), per_b()]
    args = [x, f_rows, gt]
    if norm is None:
        return pl.pallas_call(
            _resid_kernel, grid=(b, nt), in_specs=in_specs, out_specs=tok(),
            out_shape=jax.ShapeDtypeStruct((b, t, d), F32),
            compiler_params=_params(("parallel", "parallel")), name="resid",
        )(*args)
    gain, sc, sh = norm
    return pl.pallas_call(
        _resid_norm_mod_kernel, grid=(b, nt),
        in_specs=in_specs + [pl.BlockSpec((1, d), lambda bi, i: (0, 0)), per_b(), per_b()],
        out_specs=[tok(), tok()],
        out_shape=[jax.ShapeDtypeStruct((b, t, d), F32), jax.ShapeDtypeStruct((b, t, d), BF16)],
        compiler_params=_params(("parallel", "parallel")), name="resid_norm_mod",
    )(*args, gain.reshape(1, d), sc, sh)


def _norm_mod_router(x, gain, sc, sh, router_w_t, router_b, tm=512):
    b, t, d = x.shape
    tm = min(tm, t)
    e = router_w_t.shape[0]
    return pl.pallas_call(
        _norm_mod_router_kernel,
        grid=(b, t // tm),
        in_specs=[pl.BlockSpec((1, tm, d), lambda bi, i: (bi, i, 0)),
                  pl.BlockSpec((1, d), lambda bi, i: (0, 0)),
                  pl.BlockSpec((1, 1, d), lambda bi, i: (bi, 0, 0)),
                  pl.BlockSpec((1, 1, d), lambda bi, i: (bi, 0, 0)),
                  pl.BlockSpec((e, d), lambda bi, i: (0, 0)),
                  pl.BlockSpec((e, 1), lambda bi, i: (0, 0))],
        out_specs=[pl.BlockSpec((1, tm * ROW_TILES, LANES), lambda bi, i: (bi, i, 0)),
                   pl.BlockSpec((1, e, tm), lambda bi, i: (bi, 0, i))],
        out_shape=[jax.ShapeDtypeStruct((b, t * ROW_TILES, LANES), F32),
                   jax.ShapeDtypeStruct((b, e, t), F32)],
        compiler_params=_params(("parallel", "parallel")),
        name="norm_mod_router",
    )(x, gain.reshape(1, d), sc, sh, router_w_t, router_b.reshape(e, 1))


def _mm_kernel(*refs, n_terms, a_batched, w_batched, has_res):
    a_refs = refs[:n_terms]
    w_refs = refs[n_terms:2 * n_terms]
    o_ref = refs[-1]
    acc = None
    for a_ref, w_ref, ab, wb in zip(a_refs, w_refs, a_batched, w_batched):
        a = a_ref[0] if ab else a_ref[...]
        w = w_ref[0] if wb else w_ref[...]
        p = jnp.dot(a, w, preferred_element_type=F32)
        acc = p if acc is None else acc + p
    if has_res:
        res_ref, gate_ref = refs[2 * n_terms], refs[2 * n_terms + 1]
        acc = res_ref[0] + gate_ref[0] * acc
    o_ref[0] = acc.astype(o_ref.dtype)


def _matmul(a_list, w_list, out_dtype, *, batch, rows, tm, tn, res=None, gate=None, name="matmul"):
    n = w_list[0].shape[-1]
    tm = min(tm, rows)
    tn = min(tn, n)
    a_batched = tuple(a.ndim == 3 for a in a_list)
    w_batched = tuple(w.ndim == 3 for w in w_list)
    in_specs = []
    for a, ab in zip(a_list, a_batched):
        k = a.shape[-1]
        in_specs.append(pl.BlockSpec((1, tm, k), lambda b, i, j: (b, i, 0)) if ab
                        else pl.BlockSpec((tm, k), lambda b, i, j: (i, 0)))
    for w, wb in zip(w_list, w_batched):
        k = w.shape[-2]
        in_specs.append(pl.BlockSpec((1, k, tn), lambda b, i, j: (b, 0, j)) if wb
                        else pl.BlockSpec((k, tn), lambda b, i, j: (0, j)))
    args = list(a_list) + list(w_list)
    if res is not None:
        in_specs.append(pl.BlockSpec((1, tm, tn), lambda b, i, j: (b, i, j)))
        in_specs.append(pl.BlockSpec((1, 1, tn), lambda b, i, j: (b, 0, j)))
        args += [res, gate]
    kern = functools.partial(_mm_kernel, n_terms=len(a_list), a_batched=a_batched, w_batched=w_batched,
                             has_res=res is not None)
    return pl.pallas_call(
        kern,
        grid=(batch, rows // tm, n // tn),
        in_specs=in_specs,
        out_specs=pl.BlockSpec((1, tm, tn), lambda b, i, j: (b, i, j)),
        out_shape=jax.ShapeDtypeStruct((batch, rows, n), out_dtype),
        compiler_params=_params(("parallel", "parallel", "parallel")),
        name=name,
    )(*args)


def _pair_ones():
    r = lax.broadcasted_iota(jnp.int32, (LANES, LANES), 0) // RWKV_HEAD
    c = lax.broadcasted_iota(jnp.int32, (LANES, LANES), 1) // RWKV_HEAD
    return jnp.where(r == c, 1.0, 0.0).astype(BF16)


def _head_sum(x, ones_bf16):
    return _dot_exact_rhs(x, ones_bf16)


def _token_shift(ref, s, n, t, mu_p, mu_n):
    x = ref[0, s:s + n, :]
    w = x.shape[-1]
    ridx = lax.broadcasted_iota(jnp.int32, (n, w), 0)
    prev_row = ref[0, s - 1:s, :] if s > 0 else jnp.zeros((1, w), F32)
    next_row = ref[0, s + n:s + n + 1, :] if s + n < t else jnp.zeros((1, w), F32)
    prev = jnp.where(ridx == 0, prev_row, pltpu.roll(x, 1, 0))
    nxt = jnp.where(ridx == n - 1, next_row, pltpu.roll(x, n - 1, 0))
    return x + mu_p * (prev - x) + mu_n * (nxt - x)


def _rwkv_prep_kernel(pr_ref, pk_ref, pv_ref, pl_ref, par_ref, lpar_ref, w2_ref, a2_ref, g2_ref,
                      r_ref, v_ref, kk_ref, g_ref, ks_ref, lw_ref, kd_ref, alr_ref, *, t, rows):
    ones = _pair_ones()
    par = par_ref[...]
    lpar = lpar_ref[...]
    k_k, k_a = par[10:11], par[11:12]
    for s in range(0, t, rows):
        sl = slice(s, s + rows)
        r = _token_shift(pr_ref, s, rows, t, par[0:1], par[1:2])
        k = _token_shift(pk_ref, s, rows, t, par[2:3], par[3:4])
        v = _token_shift(pv_ref, s, rows, t, par[4:5], par[5:6])
        lo = _token_shift(pl_ref, s, rows, t, lpar[0:1], lpar[1:2])
        xwa, xg = lo[:, :LANES], lo[:, LANES:]
        r_ref[0, sl, :] = r
        v_ref[0, sl, :] = v
        g_ref[0, sl, :] = _bdot(_sigmoid(xg), g2_ref[...])
        kk = k * k_k
        kk = kk * lax.rsqrt(_head_sum(kk * kk, ones) + KK_EPS)
        kk_ref[0, sl, :] = kk
        tw = jnp.tanh(xwa)
        xa = xwa
        ksum = None
        for d in range(2):
            z = -(par[6 + d:7 + d] + _bdot(tw, w2_ref[d]))
            softplus = jnp.maximum(z, 0.0) + jnp.log(1.0 + jnp.exp(-jnp.abs(z)))
            w_log = -softplus - 0.5
            lw_ref[d, 0, sl, :] = -jnp.exp(w_log)
            a = _sigmoid(par[8 + d:9 + d] + _bdot(xa, a2_ref[d]))
            alr_ref[d, 0, sl, :] = a
            k_d = k * (1.0 + (a - 1.0) * k_a)
            kd_ref[d, 0, sl, :] = k_d
            ksum = k_d if ksum is None else ksum + k_d
        ks_ref[0, sl, :] = ksum


def _rwkv_prep(pr, pk, pv, plo, par, lpar, w2, a2, g2):
    b, t, _ = pr.shape
    npair = D_A // LANES
    rows = min(256, t)
    lora = plo.shape[-1]
    slab = lambda: pl.BlockSpec((1, t, LANES), lambda bi, p: (bi, 0, p))
    dslab = lambda: pl.BlockSpec((2, 1, t, LANES), lambda bi, p: (0, bi, 0, p))
    one = jax.ShapeDtypeStruct((b, t, D_A), F32)
    two = jax.ShapeDtypeStruct((2, b, t, D_A), F32)
    return pl.pallas_call(
        functools.partial(_rwkv_prep_kernel, t=t, rows=rows),
        grid=(b, npair),
        in_specs=[slab(), slab(), slab(),
                  pl.BlockSpec((1, t, lora), lambda bi, p: (bi, 0, 0)),
                  pl.BlockSpec((16, LANES), lambda bi, p: (0, p)),
                  pl.BlockSpec((8, lora), lambda bi, p: (0, 0)),
                  pl.BlockSpec((2, LANES, LANES), lambda bi, p: (0, 0, p)),
                  pl.BlockSpec((2, LANES, LANES), lambda bi, p: (0, 0, p)),
                  pl.BlockSpec((G_LORA, LANES), lambda bi, p: (0, p))],
        out_specs=[slab(), slab(), slab(), slab(), slab(), dslab(), dslab(), dslab()],
        out_shape=[one, one, one, one, one, two, two, two],
        compiler_params=_params(("parallel", "parallel")),
        name="rwkv_prep",
    )(pr, pk, pv, plo, par, lpar, w2, a2, g2)


_gram_dot = _bdot
_inv_dot = _bdot
_wu_dot = _bdot
_state_dot = _bdot


def _wkv_local(chunks, sgn, consts):
    n = WKV_CHUNK
    n2 = 2 * n
    tri, keep, dt, eye, lvl = consts
    order = sgn * dt
    strict = order > 0
    incl = order >= 0
    tri_b = jnp.where(sgn * tri >= 0, 1.0, 0.0).astype(BF16)
    each = lambda f, *ls: [f(*xs) for xs in zip(*ls)]

    def stack(x):
        return jnp.where(keep, jnp.concatenate([x, x], axis=0), 0.0)

    lws = [c[1] for c in chunks]
    cums = each(lambda lw: _dot_exact_lhs(tri_b, lw), lws)
    ctots = each(lambda lw: jnp.sum(lw, axis=0, keepdims=True), lws)
    lhs, rhs, kh2s, bh2s, v2s, at2s, rt2s = [], [], [], [], [], [], []
    for (r, lw, k, v, kk, alr), cum, ctot in zip(chunks, cums, ctots):
        e_neg = jnp.exp(-cum)
        e_rem = jnp.exp(ctot - cum)
        b = kk * alr
        rt2 = stack(r * jnp.exp(cum))
        at2 = stack(-kk * jnp.exp(cum - lw))
        lhs.append(jnp.concatenate([at2, rt2], axis=0))
        rhs.append(jnp.concatenate([stack(b * e_neg), stack(k * e_neg)], axis=0))
        kh2s.append(stack(k * e_rem))
        bh2s.append(stack(b * e_rem))
        v2s.append(stack(v))
        at2s.append(at2)
        rt2s.append(rt2)
    gs = each(lambda x, y: _gram_dot(x, y, NT_DIMS), lhs, rhs)
    a_abs = [jnp.where(strict, g[:n2, :n2], 0.0) for g in gs]
    a_rbs = [jnp.where(incl, g[n2:, :n2], 0.0) for g in gs]
    a_kks = [jnp.concatenate([jnp.where(strict, g[:n2, n2:], 0.0), jnp.where(incl, g[n2:, n2:], 0.0)], axis=0)
             for g in gs]
    m0 = sgn * lvl[0] == 1
    tinvs = [eye + jnp.where(m0, a, 0.0) for a in a_abs]
    for m in lvl[1:]:
        msk = sgn * m == 1
        cts = each(lambda a, t: _inv_dot(jnp.where(msk, a, 0.0), t), a_abs, tinvs)
        tinvs = each(lambda t, ct: t + _inv_dot(t, ct), tinvs, cts)
    kvs = each(_bdot, a_kks, v2s)
    wus = each(lambda t, at2, kv: _wu_dot(t, jnp.concatenate([at2, kv[:n2]], axis=1)), tinvs, at2s, kvs)
    rys = each(lambda rt2, kv, a_rb, wu: jnp.concatenate([rt2, kv[n2:]], axis=1) + _bdot(a_rb, wu),
               rt2s, kvs, a_rbs, wus)
    bws = each(lambda bh2, wu: _bdot(bh2, wu, TN_DIMS), bh2s, wus)
    khv = each(lambda kh2, v2: _bdot(kh2, v2, TN_DIMS), kh2s, v2s)
    out = []
    for ry, bw, kv, ctot in zip(rys, bws, khv, ctots):
        decay_col = jnp.sum(jnp.where(eye > 0, jnp.exp(ctot), 0.0), axis=1, keepdims=True)
        out.append((ry[:, :LANES], ry[:, LANES:], bw[:, :LANES], bw[:, LANES:] + kv, decay_col))
    return out


def _wkv_consts():
    n = WKV_CHUNK
    n2 = 2 * n
    tri = lax.broadcasted_iota(jnp.int32, (n, n), 0) - lax.broadcasted_iota(jnp.int32, (n, n), 1)
    lane = lax.broadcasted_iota(jnp.int32, (n2, LANES), 1)
    rr = lax.broadcasted_iota(jnp.int32, (n2, LANES), 0)
    keep = (lane >= RWKV_HEAD) == (rr >= n)
    ri = lax.broadcasted_iota(jnp.int32, (n2, n2), 0)
    ci = lax.broadcasted_iota(jnp.int32, (n2, n2), 1)
    r2 = ri & (n - 1)
    c2 = ci & (n - 1)
    dt = r2 - c2
    eye = jnp.where(ri == ci, 1.0, 0.0).astype(F32)
    lvl = []
    k = 0
    while (1 << k) < n:
        same = (r2 >> (k + 1)) == (c2 >> (k + 1))
        diff = ((r2 >> k) & 1) - ((c2 >> k) & 1)
        lvl.append(jnp.where(same, diff, 0))
        k += 1
    return tri, keep, dt, eye, lvl


def _wkv_kernel(r_ref, v_ref, kk_ref, lw_ref, kd_ref, alr_ref, s0_ref, y_ref, sfin_ref, st_ref, *,
                n_sub, n_pairs):
    d = pl.program_id(0)
    c = pl.program_id(3)
    nc = pl.num_programs(3)

    @pl.when(c == 0)
    def _():
        st_ref[...] = s0_ref[0, 0]

    sgn = 1 - 2 * d
    rows = []
    for u in range(n_sub):
        ui = u + d * (n_sub - 1 - 2 * u)
        rows.append(pl.ds(pl.multiple_of(ui * WKV_CHUNK, WKV_CHUNK), WKV_CHUNK))
    lanes = [slice(p * LANES, (p + 1) * LANES) for p in range(n_pairs)]
    chunks = [(r_ref[0, rw, ln], lw_ref[0, 0, rw, ln], kd_ref[0, 0, rw, ln], v_ref[0, rw, ln], kk_ref[0, rw, ln],
               alr_ref[0, 0, rw, ln]) for ln in lanes for rw in rows]
    local = _wkv_local(chunks, sgn, _wkv_consts())
    for p, ln in enumerate(lanes):
        st = st_ref[p]
        for rw, (rhat, yhat, mmat, nmat, decay_col) in zip(rows, local[p * n_sub:(p + 1) * n_sub]):
            y2 = _state_dot(rhat, st) + yhat
            y_ref[0, 0, rw, ln] = y2[:WKV_CHUNK] + y2[WKV_CHUNK:]
            st = decay_col * st + _state_dot(mmat, st) + nmat
        st_ref[p] = st

        @pl.when(c == nc - 1)
        def _():
            sfin_ref[0, 0, p] = st


def _wkv(r, v, kk, lw, kd, alr, s0):
    b, t, _ = r.shape
    npair = D_A // LANES
    n_sub = min(WKV_SUB, t // WKV_CHUNK)
    n_pairs = WKV_SUB // n_sub
    blk = n_sub * WKV_CHUNK
    nc = t // blk
    wl = n_pairs * LANES

    def tmap(d, bi, p, c):
        return (bi, c + d * (nc - 1 - 2 * c), p)

    def dmap(d, bi, p, c):
        return (d, bi, c + d * (nc - 1 - 2 * c), p)

    smap = lambda d, bi, p, c: (d, bi, p, 0, 0)
    shared = lambda: pl.BlockSpec((1, blk, wl), tmap)
    direc = lambda: pl.BlockSpec((1, 1, blk, wl), dmap)
    state = lambda: pl.BlockSpec((1, 1, n_pairs, LANES, LANES), smap)
    return pl.pallas_call(
        functools.partial(_wkv_kernel, n_sub=n_sub, n_pairs=n_pairs),
        grid=(2, b, npair // n_pairs, nc),
        in_specs=[shared(), shared(), shared(), direc(), direc(), direc(), state()],
        out_specs=[direc(), state()],
        out_shape=[jax.ShapeDtypeStruct((2, b, t, D_A), F32),
                   jax.ShapeDtypeStruct((2, b, npair, LANES, LANES), F32)],
        scratch_shapes=[pltpu.VMEM((n_pairs, LANES, LANES), F32)],
        compiler_params=_params(("parallel", "parallel", "parallel", "arbitrary")),
        name="wkv",
    )(r, v, kk, lw, kd, alr, s0)


def _rwkv_finish_kernel(y_ref, r_ref, v_ref, ks_ref, g_ref, par_ref, o_ref):
    ones = _pair_ones()
    par = par_ref[...]
    inv_n = 1.0 / RWKV_HEAD
    for p in range(D_A // LANES):
        sl = slice(p * LANES, (p + 1) * LANES)
        y = y_ref[0, 0, :, sl] + y_ref[1, 0, :, sl]
        mu = _head_sum(y, ones) * inv_n
        yc = y - mu
        var = _head_sum(yc * yc, ones) * inv_n
        yn = yc * lax.rsqrt(var + LNX_EPS) * par[0:1, sl] + par[1:2, sl]
        bonus = _head_sum(r_ref[0, :, sl] * ks_ref[0, :, sl] * par[2:3, sl], ones) * v_ref[0, :, sl]
        o_ref[0, :, sl] = ((yn + bonus) * g_ref[0, :, sl]).astype(BF16)


def _rwkv_finish(y, r, v, ks, g, par, tm=256):
    b, t, d = r.shape
    tile = lambda: pl.BlockSpec((1, tm, d), lambda bi, i: (bi, i, 0))
    return pl.pallas_call(
        _rwkv_finish_kernel,
        grid=(b, t // tm),
        in_specs=[pl.BlockSpec((2, 1, tm, d), lambda bi, i: (0, bi, i, 0)), tile(), tile(), tile(), tile(),
                  pl.BlockSpec((8, d), lambda bi, i: (0, 0))],
        out_specs=tile(),
        out_shape=jax.ShapeDtypeStruct((b, t, d), BF16),
        compiler_params=_params(("parallel", "parallel")),
        name="rwkv_finish",
    )(y, r, v, ks, g, par)


def _head_rms(x, gain):
    return x * lax.rsqrt(jnp.mean(x * x, axis=-1, keepdims=True) + NORM_EPS) * gain


def _rope(x, cos_e, sin_s):
    lane = lax.broadcasted_iota(jnp.int32, x.shape, 1)
    swapped = jnp.where((lane & 1) == 0, pltpu.roll(x, LANES - 1, 1), pltpu.roll(x, 1, 1))
    return x * cos_e + swapped * sin_s


def _attn_prep_kernel(*refs, with_q):
    if with_q:
        pq_ref, pkv_ref, gains_ref, cos_ref, sin_ref, q_ref, k_ref, v_ref = refs
    else:
        pkv_ref, gains_ref, k_ref, v_ref = refs
    gains = gains_ref[...]
    kv_w = H_KV * HEAD_DIM
    for h in range(H_KV):
        sl = slice(h * HEAD_DIM, (h + 1) * HEAD_DIM)
        kh = _head_rms(pkv_ref[0, :, sl], gains[1:2])
        if with_q:
            kh = _rope(kh, cos_ref[...], sin_ref[...])
        k_ref[0, :, sl] = kh.astype(BF16)
    v_ref[0] = pkv_ref[0, :, kv_w:].astype(BF16)
    if with_q:
        scale = HEAD_DIM ** -0.5 * float(np.log2(np.e))
        for h in range(H_Q):
            sl = slice(h * HEAD_DIM, (h + 1) * HEAD_DIM)
            qh = _rope(_head_rms(pq_ref[0, :, sl], gains[0:1]), cos_ref[...], sin_ref[...])
            q_ref[0, :, sl] = (qh * scale).astype(BF16)


def _attn_prep(pq, pkv, gains, cos_e, sin_s, tm=256):
    b, t, _ = pkv.shape
    with_q = pq is not None
    kv_w = H_KV * HEAD_DIM
    tm = min(tm, t)
    kv_spec = lambda: pl.BlockSpec((1, tm, kv_w), lambda bi, i: (bi, i, 0))
    in_specs = [pl.BlockSpec((1, tm, 2 * kv_w), lambda bi, i: (bi, i, 0)),
                pl.BlockSpec((8, HEAD_DIM), lambda bi, i: (0, 0))]
    args = [pkv, gains]
    out_specs = [kv_spec(), kv_spec()]
    out_shape = [jax.ShapeDtypeStruct((b, t, kv_w), BF16)] * 2
    if with_q:
        in_specs = [pl.BlockSpec((1, tm, D_B), lambda bi, i: (bi, i, 0))] + in_specs + [
            pl.BlockSpec((tm, HEAD_DIM), lambda bi, i: (i, 0)),
            pl.BlockSpec((tm, HEAD_DIM), lambda bi, i: (i, 0))]
        args = [pq] + args + [cos_e, sin_s]
        out_specs = [pl.BlockSpec((1, tm, D_B), lambda bi, i: (bi, i, 0))] + out_specs
        out_shape = [jax.ShapeDtypeStruct((b, t, D_B), BF16)] + out_shape
    return pl.pallas_call(
        functools.partial(_attn_prep_kernel, with_q=with_q),
        grid=(b, t // tm),
        in_specs=in_specs,
        out_specs=out_specs,
        out_shape=out_shape,
        compiler_params=_params(("parallel", "parallel")),
        name="attn_prep",
    )(*args)


def _attn_kernel(q_ref, k_ref, v_ref, o_ref):
    k = k_ref[0]
    v = v_ref[0]
    for g in range(Q_PER_KV):
        sl = slice(g * HEAD_DIM, (g + 1) * HEAD_DIM)
        s = lax.dot_general(q_ref[0, :, sl], k, NT_DIMS, preferred_element_type=F32)
        p = jnp.exp2(s - jnp.max(s, axis=-1, keepdims=True))
        o = jnp.dot(p.astype(BF16), v, preferred_element_type=F32)
        o_ref[0, :, sl] = (o[:, :HEAD_DIM] / o[:, HEAD_DIM:HEAD_DIM + 1]).astype(BF16)


def _attention(q, k_all, v_all, tq=256):
    b, t, _ = q.shape
    s_len = k_all.shape[1]
    gw = Q_PER_KV * HEAD_DIM
    return pl.pallas_call(
        _attn_kernel,
        grid=(b, H_KV, t // tq),
        in_specs=[pl.BlockSpec((1, tq, gw), lambda bi, h, i: (bi, i, h)),
                  pl.BlockSpec((1, s_len, HEAD_DIM), lambda bi, h, i: (bi, 0, h)),
                  pl.BlockSpec((1, s_len, 2 * HEAD_DIM), lambda bi, h, i: (bi, 0, h))],
        out_specs=pl.BlockSpec((1, tq, gw), lambda bi, h, i: (bi, i, h)),
        out_shape=jax.ShapeDtypeStruct((b, t, H_Q * HEAD_DIM), BF16),
        compiler_params=_params(("parallel", "parallel", "parallel")),
        name="attention",
    )(q, k_all, v_all)


def _dft_channels_kernel(h_ref, w_ref, c_ref, s_ref):
    p = jnp.dot(h_ref[0], w_ref[...], preferred_element_type=F32)
    c_ref[0] = p[:, :FOURIER_GROUP_DIM].astype(BF16)
    s_ref[0] = p[:, FOURIER_GROUP_DIM:].astype(BF16)


def _dft_channels(h, w_cs, tm=512):
    b, t, d = h.shape
    tm = min(tm, t)
    gd = FOURIER_GROUP_DIM
    blk = lambda: pl.BlockSpec((1, tm, gd), lambda bi, i, g: (bi, i, g))
    return pl.pallas_call(
        _dft_channels_kernel,
        grid=(b, t // tm, d // gd),
        in_specs=[blk(), pl.BlockSpec((gd, 2 * gd), lambda bi, i, g: (0, 0))],
        out_specs=[blk(), blk()],
        out_shape=[jax.ShapeDtypeStruct((b, t, d), BF16)] * 2,
        compiler_params=_params(("parallel", "parallel", "parallel")),
        name="dft_channels",
    )(h, w_cs)


def _dft_mats(n):
    i = lax.broadcasted_iota(jnp.int32, (n, n), 0)
    j = lax.broadcasted_iota(jnp.int32, (n, n), 1)
    ang = ((i * j) % n).astype(F32) * (2.0 * np.pi / n)
    scale = n ** -0.5
    return jnp.cos(ang) * scale, jnp.sin(ang) * scale


MOE_TILE = 512


def _moe_kernel(src_ref, grp_ref, live_ref, h_hbm, gs_ref, wg_ref, wu_ref, wd_ref, o_hbm,
                rows_ref, h16_ref, acc_ref, gsem, ssem, *, tm, n_tok):
    i = pl.program_id(0)
    j = pl.program_id(1)
    n_live = live_ref[0]
    live = i < n_live
    slot = i % 2

    def row(ref, t):
        return ref.at[pl.ds(pl.multiple_of(t * ROW_TILES, ROW_TILES), ROW_TILES)]

    def start_gather(tile, sl):
        def issue(k, c):
            tok = jnp.maximum(src_ref[tile * tm + k], 0)
            pltpu.make_async_copy(row(h_hbm, tok), row(rows_ref.at[sl], k), gsem.at[sl]).start()
            return c
        lax.fori_loop(0, tm, issue, 0, unroll=8)

    def wait_gather(sl):
        pltpu.make_async_copy(h_hbm.at[pl.ds(0, tm * ROW_TILES)], rows_ref.at[sl], gsem.at[sl]).wait()

    def start_scatter(tile, sl):
        def issue(k, c):
            tok = src_ref[tile * tm + k]
            tok = jnp.where(tok < 0, n_tok + k, tok)
            pltpu.make_async_copy(row(rows_ref.at[sl], k), row(o_hbm, tok), ssem.at[sl]).start()
            return c
        lax.fori_loop(0, tm, issue, 0, unroll=8)

    def wait_scatter(sl):
        pltpu.make_async_copy(rows_ref.at[sl], o_hbm.at[pl.ds(0, tm * ROW_TILES)], ssem.at[sl]).wait()

    @pl.when(jnp.logical_and(live, j == 0))
    def _():
        @pl.when(i == 0)
        def _():
            start_gather(0, 0)
            rows_ref[1] = jnp.zeros(rows_ref.shape[1:], F32)
            spare = pltpu.make_async_copy(rows_ref.at[1], o_hbm.at[pl.ds(n_tok * ROW_TILES, tm * ROW_TILES)],
                                          ssem.at[1])
            spare.start()
            spare.wait()
        wait_gather(slot)
        h16_ref[...] = _from_token_rows(rows_ref, tm, lead=(slot,)).astype(BF16)
        acc_ref[...] = jnp.zeros_like(acc_ref)

    @pl.when(jnp.logical_and(live, j == 1))
    def _():
        @pl.when(i > 0)
        def _():
            wait_scatter(1 - slot)

        @pl.when(i + 1 < n_live)
        def _():
            start_gather(i + 1, 1 - slot)

    @pl.when(live)
    def _():
        h = h16_ref[...]
        u = _silu(_bdot(h, wg_ref[0, 0])) * _bdot(h, wu_ref[0, 0])
        lane = lax.broadcasted_iota(jnp.int32, gs_ref.shape, 1)
        gcol = jnp.sum(jnp.where(lane == j, gs_ref[...], 0.0), axis=1, keepdims=True)
        acc_ref[...] += gcol * _bdot(u, wd_ref[0, 0])

    @pl.when(jnp.logical_and(live, j == pl.num_programs(1) - 1))
    def _():
        _to_token_rows(rows_ref, acc_ref[...], lead=(slot,))
        start_scatter(i, slot)

        @pl.when(i == n_live - 1)
        def _():
            wait_scatter(slot)


def _route_meta(gates, tm):
    n = gates.shape[0]
    ng, epg = N_EXPERT_GROUPS, EXPERTS_PER_GROUP
    gg = gates.reshape(n, ng, epg)
    grp = jnp.argmax(jnp.max(gg, axis=-1) > 0, axis=-1).astype(jnp.int32)
    onehot = (grp[:, None] == jnp.arange(ng)[None, :]).astype(jnp.int32)
    counts = jnp.sum(onehot, axis=0)
    padded = ((counts + tm - 1) // tm) * tm
    ends = jnp.cumsum(padded)
    starts = ends - padded
    rank = jnp.sum((jnp.cumsum(onehot, axis=0) - onehot) * onehot, axis=1)
    pos = jnp.sum(starts[None, :] * onehot, axis=1) + rank
    p_total = n + ng * tm
    src = jnp.full((p_total,), -1, jnp.int32).at[pos].set(jnp.arange(n, dtype=jnp.int32))
    gsel = jnp.sum(gg * onehot[:, :, None].astype(F32), axis=1)
    gs_sorted = jnp.zeros((p_total, epg), F32).at[pos].set(gsel)
    n_live = ends[-1] // tm
    tile_start = jnp.minimum(jnp.arange(p_total // tm, dtype=jnp.int32), n_live - 1) * tm
    tile_grp = jnp.minimum(jnp.sum(tile_start[:, None] >= ends[None, :], axis=1), ng - 1).astype(jnp.int32)
    return src, gs_sorted, tile_grp, n_live.astype(jnp.int32).reshape(1)


def _moe(h_rows, gates, wg, wu, wd, layer, tm=MOE_TILE):
    n = gates.shape[0]
    d = D_MODEL
    _, ne, _, de = wg.shape
    epg = EXPERTS_PER_GROUP
    src, gs_sorted, tile_grp, n_live = _route_meta(gates, tm)
    n_tiles = src.shape[0] // tm

    def wmap(i, j, src, grp, live):
        return (layer, grp[i] * epg + jnp.where(i < live[0], j, epg - 1), 0, 0)

    grid_spec = pltpu.PrefetchScalarGridSpec(
        num_scalar_prefetch=3,
        grid=(n_tiles, epg),
        in_specs=[pl.BlockSpec(memory_space=pl.ANY),
                  pl.BlockSpec((tm, epg), lambda i, j, src, grp, live: (i, 0)),
                  pl.BlockSpec((1, 1, d, de), wmap),
                  pl.BlockSpec((1, 1, d, de), wmap),
                  pl.BlockSpec((1, 1, de, d), wmap)],
        out_specs=pl.BlockSpec(memory_space=pl.ANY),
        scratch_shapes=[pltpu.VMEM((2, tm * ROW_TILES, LANES), F32), pltpu.VMEM((tm, d), BF16),
                        pltpu.VMEM((tm, d), F32), pltpu.SemaphoreType.DMA((2,)), pltpu.SemaphoreType.DMA((2,))],
    )
    return pl.pallas_call(
        functools.partial(_moe_kernel, tm=tm, n_tok=n),
        grid_spec=grid_spec,
        out_shape=jax.ShapeDtypeStruct(((n + tm) * ROW_TILES, LANES), F32),
        compiler_params=pltpu.CompilerParams(dimension_semantics=("arbitrary", "arbitrary"),
                                             vmem_limit_bytes=VMEM_LIMIT, disable_bounds_checks=True),
        name="moe",
    )(src, tile_grp, n_live, h_rows, gs_sorted, wg, wu, wd)


def _rope_tables(n):
    rows = n // GRID_W
    row = jnp.repeat(jnp.arange(rows), GRID_W).astype(F32)
    col = jnp.tile(jnp.arange(GRID_W), rows).astype(F32)
    inv = ROPE_THETA ** (-jnp.arange(0, ROPE_AXIS_DIM, 2, dtype=F32) / ROPE_AXIS_DIM)
    ang = jnp.concatenate([row[:, None] * inv, col[:, None] * inv], axis=-1)
    cos_e = jnp.repeat(jnp.cos(ang), 2, axis=-1)
    sin_s = jnp.stack([-jnp.sin(ang), jnp.sin(ang)], axis=-1).reshape(n, HEAD_DIM)
    return cos_e, sin_s


def _pad_rows(rows, n):
    a = jnp.stack(rows, axis=0)
    return jnp.pad(a, ((0, n - a.shape[0]), (0, 0)))


def _enter_layer(x, pending, gain, sc, sh):
    if pending is None:
        return x, _norm_mod(x, gain, sc, sh)
    f_rows, gt = pending
    return _resid(x, f_rows, gt, norm=(gain, sc, sh))


def _even_mixer(x, pending, ctx, mods, gain, w_in, mu_prev, mu_next, w0, w2, a0, a2, g2, k_k, k_a, r_k, ln_g, ln_b,
                q_gain, k_gain, w_out):
    bsz, seq, _ = x.shape
    sh1, sc1, gt1, csh1, csc1 = mods
    x, h_lat = _enter_layer(x, pending, gain, sc1, sh1)
    h_ctx = _norm_mod(ctx, gain, csc1, csh1)

    a_cols = 3 * D_A + W_LORA + A_LORA + G_LORA
    wb = w_in.astype(BF16)
    w_r, w_k, w_v = wb[:, :D_A], wb[:, D_A:2 * D_A], wb[:, 2 * D_A:3 * D_A]
    w_lo = wb[:, 3 * D_A:a_cols]
    w_q = wb[:, a_cols:a_cols + D_B]
    w_kv = wb[:, a_cols + D_B:]

    def proj(h, w, name):
        rows = h.shape[1]
        return _matmul([h], [w], F32, batch=bsz, rows=rows, tm=1024, tn=512, name=name)

    par = _pad_rows([mu_prev[:D_A], mu_next[:D_A], mu_prev[D_A:2 * D_A], mu_next[D_A:2 * D_A],
                     mu_prev[2 * D_A:3 * D_A], mu_next[2 * D_A:3 * D_A],
                     w0[0], w0[1], a0[0], a0[1], k_k, k_a], 16)
    lpar = _pad_rows([mu_prev[3 * D_A:], mu_next[3 * D_A:]], 8)
    fin_par = _pad_rows([ln_g, ln_b, r_k.reshape(-1)], 8)
    w2 = jnp.pad(w2, ((0, 0), (0, A_LORA), (0, 0)))
    a2 = jnp.pad(a2, ((0, 0), (W_LORA, 0), (0, 0)))

    def streams(h):
        return _rwkv_prep(proj(h, w_r, "proj_r"), proj(h, w_k, "proj_k"), proj(h, w_v, "proj_v"),
                          proj(h, w_lo, "proj_lora"), par, lpar, w2, a2, g2)

    rc, vc, kkc, _, _, lwc, kdc, alrc = streams(h_ctx)
    rl, vl, kkl, gl, ksl, lwl, kdl, alrl = streams(h_lat)
    s0 = jnp.zeros((2, bsz, D_A // LANES, LANES, LANES), F32)
    _, s_ctx = _wkv(rc, vc, kkc, lwc, kdc, alrc, s0)
    y, _ = _wkv(rl, vl, kkl, lwl, kdl, alrl, s_ctx)
    ya = _rwkv_finish(y, rl, vl, ksl, gl, fin_par)

    cos_e, sin_s = _rope_tables(seq)
    gains = _pad_rows([q_gain, k_gain], 8)
    k_c, v_c = _attn_prep(None, proj(h_ctx, w_kv, "proj_kv"), gains, None, None)
    q_l, k_l, v_l = _attn_prep(proj(h_lat, w_q, "proj_q"), proj(h_lat, w_kv, "proj_kv"), gains, cos_e, sin_s)
    v_all = jnp.concatenate([v_c, v_l], axis=1).reshape(bsz, -1, H_KV, HEAD_DIM)
    v_ext = jnp.concatenate([v_all, jnp.ones_like(v_all)], axis=-1).reshape(bsz, -1, 2 * H_KV * HEAD_DIM)
    yb = _attention(q_l, jnp.concatenate([k_c, k_l], axis=1), v_ext)

    wo = w_out.astype(BF16)
    return _matmul([ya, yb], [wo[:D_A], wo[D_A:]], F32, batch=bsz, rows=seq, tm=1024, tn=512,
                   res=x, gate=gt1, name="mixer_out")


def _fourier_mixer(x, pending, mods, gain, w_out):
    bsz, seq, d = x.shape
    sh1, sc1, gt1 = mods
    x, h = _enter_layer(x, pending, gain, sc1, sh1)
    cc, sc = _dft_mats(FOURIER_GROUP_DIM)
    ct, st = _dft_mats(seq)
    pc, ps = _dft_channels(h, jnp.concatenate([cc, sc], axis=1).astype(BF16))
    f = _matmul([ct.astype(BF16), (-st).astype(BF16)], [pc, ps], BF16, batch=bsz, rows=seq, tm=512, tn=512,
                name="dft_tokens")
    return _matmul([f], [w_out.astype(BF16)], F32, batch=bsz, rows=seq, tm=1024, tn=512, res=x, gate=gt1,
                   name="mixer_out")


def _moe_layer(x, mods, gain, router_w_t, router_b, w_gate, w_up, w_down, layer):
    bsz, seq, d = x.shape
    sh2, sc2, _ = mods
    h_rows, gates_t = _norm_mod_router(x, gain, sc2, sh2, router_w_t, router_b)
    gates = jnp.swapaxes(gates_t, 1, 2).reshape(bsz * seq, -1)
    return _moe(h_rows.reshape(bsz * seq * ROW_TILES, LANES), gates, w_gate, w_up, w_down, layer)


def kernel(x, c, ctx, c_ctx, mod_w, mod_b, norm_mix, norm_ffn, ab_w_in, ab_mu_prev, ab_mu_next,
           rwkv_w0, rwkv_w2, rwkv_a0, rwkv_a2, rwkv_g2, rwkv_k_k, rwkv_k_a, rwkv_r_k, rwkv_ln_g, rwkv_ln_b,
           attn_q_norm, attn_k_norm, ab_w_out, fourier_w_out, router_w, router_b,
           exp_w_gate, exp_w_up, exp_w_down):
    bsz, seq, d = x.shape
    depth = mod_w.shape[0]
    cvec = jnp.concatenate([c, c_ctx[None], jnp.zeros((8 - bsz - 1, d), F32)], axis=0)
    mod = _modulation(cvec, mod_w, mod_b)
    router_w_t = router_w.T
    pending = None
    for l in range(depth):
        m_lat = mod[l, :bsz, None, :]
        sh1, sc1, gt1, sh2, sc2, gt2 = jnp.split(m_lat, 6, axis=-1)
        if l % 2 == 0:
            e = l // 2
            m_ctx = mod[l, bsz:bsz + 1, None, :]
            csh1, csc1 = m_ctx[..., :d], m_ctx[..., d:2 * d]
            x = _even_mixer(x, pending, ctx, (sh1, sc1, gt1, csh1, csc1), norm_mix[l], ab_w_in[e], ab_mu_prev[e],
                            ab_mu_next[e], rwkv_w0[e], rwkv_w2[e], rwkv_a0[e], rwkv_a2[e], rwkv_g2[e],
                            rwkv_k_k[e], rwkv_k_a[e], rwkv_r_k[e], rwkv_ln_g[e], rwkv_ln_b[e],
                            attn_q_norm[e], attn_k_norm[e], ab_w_out[e])
        else:
            x = _fourier_mixer(x, pending, (sh1, sc1, gt1), norm_mix[l], fourier_w_out[l // 2])
        f_rows = _moe_layer(x, (sh2, sc2, gt2), norm_ffn[l], router_w_t, router_b,
                            exp_w_gate, exp_w_up, exp_w_down, l)
        pending = (f_rows, gt2)
    return _resid(x, pending[0], pending[1])
```

```python
import functools

import jax
import jax.numpy as jnp
import numpy as np
from jax import lax
from jax.experimental import pallas as pl
from jax.experimental.pallas import tpu as pltpu

F32 = jnp.float32
BF16 = jnp.bfloat16

D_MODEL = 2048
D_A = 1024
RWKV_HEAD = 64
W_LORA = 64
A_LORA = 64
G_LORA = 128
LNX_EPS = 64e-5
D_B = 1024
HEAD_DIM = 128
H_Q = 8
H_KV = 2
Q_PER_KV = H_Q // H_KV
GRID_W = 64
ROPE_THETA = 10000.0
ROPE_AXIS_DIM = HEAD_DIM // 2
FOURIER_GROUPS = 8
FOURIER_GROUP_DIM = D_MODEL // FOURIER_GROUPS
N_EXPERTS = 16
N_EXPERT_GROUPS = 4
EXPERTS_PER_GROUP = 4
D_EXPERT = 512
NORM_EPS = 1e-6
KK_EPS = 1e-12

LANES = 128
WKV_CHUNK = 64
WKV_SUB = 16
VMEM_LIMIT = 56 * 1024 * 1024

NT_DIMS = (((1,), (1,)), ((), ()))
TN_DIMS = (((0,), (0,)), ((), ()))


def _params(sem):
    return pltpu.CompilerParams(dimension_semantics=sem, vmem_limit_bytes=VMEM_LIMIT)


def _bdot(a, b, dims=None):
    a = a.astype(BF16)
    b = b.astype(BF16)
    if dims is None:
        return jnp.dot(a, b, preferred_element_type=F32)
    return lax.dot_general(a, b, dims, preferred_element_type=F32)


def _split2(x):
    hi = x.astype(BF16)
    lo = (x - hi.astype(F32)).astype(BF16)
    return hi, lo


def _split3(x):
    hi = x.astype(BF16)
    r1 = x - hi.astype(F32)
    mid = r1.astype(BF16)
    lo = (r1 - mid.astype(F32)).astype(BF16)
    return hi, mid, lo


def _dot3(a, b, dims=None):
    ah, al = _split2(a)
    bh, bl = _split2(b)
    return _bdot(ah, bh, dims) + (_bdot(ah, bl, dims) + _bdot(al, bh, dims))


def _dot_exact_lhs(a_bf16, x, dims=None):
    h, m, l = _split3(x)
    return _bdot(a_bf16, h, dims) + (_bdot(a_bf16, m, dims) + _bdot(a_bf16, l, dims))


def _dot_exact_rhs(x, b_bf16):
    h, l = _split2(x)
    return _bdot(h, b_bf16) + _bdot(l, b_bf16)


def _sigmoid(x):
    return 1.0 / (1.0 + jnp.exp(-x))


def _silu(x):
    return x * _sigmoid(x)


def _mod_kernel(c_ref, w_ref, b_ref, o_ref):
    s = _silu(c_ref[...])
    o_ref[0] = _bdot(s, w_ref[0]) + b_ref[0]


def _modulation(cvec, mod_w, mod_b):
    depth, d, n = mod_w.shape
    tn = 1024
    return pl.pallas_call(
        _mod_kernel,
        grid=(depth, n // tn),
        in_specs=[pl.BlockSpec((8, d), lambda l, j: (0, 0)),
                  pl.BlockSpec((1, d, tn), lambda l, j: (l, 0, j)),
                  pl.BlockSpec((1, 1, tn), lambda l, j: (l, 0, j))],
        out_specs=pl.BlockSpec((1, 8, tn), lambda l, j: (l, 0, j)),
        out_shape=jax.ShapeDtypeStruct((depth, 8, n), F32),
        compiler_params=_params(("parallel", "parallel")),
        name="modulation",
    )(cvec, mod_w, mod_b.reshape(depth, 1, n))


def _normed(x, gain, sc, sh):
    y = x * lax.rsqrt(jnp.mean(x * x, axis=-1, keepdims=True) + NORM_EPS)
    return (y * gain) * (1.0 + sc) + sh


def _norm_mod_kernel(x_ref, g_ref, sc_ref, sh_ref, o_ref):
    o_ref[0] = _normed(x_ref[0], g_ref[...], sc_ref[0], sh_ref[0]).astype(BF16)


def _top2_gates(logits, bias):
    s = _sigmoid(logits)
    sel = s + bias
    neg = jnp.float32(-jnp.inf)
    flags, scores = [], []
    for g in range(N_EXPERT_GROUPS):
        xs = [sel[g * EXPERTS_PER_GROUP + i:g * EXPERTS_PER_GROUP + i + 1] for i in range(EXPERTS_PER_GROUP)]
        m1 = functools.reduce(jnp.maximum, xs)
        first, taken = [], None
        for x in xs:
            f = (x == m1) if taken is None else jnp.logical_and(x == m1, jnp.logical_not(taken))
            taken = f if taken is None else jnp.logical_or(taken, f)
            first.append(f)
        rest = [jnp.where(f, neg, x) for f, x in zip(first, xs)]
        m2 = functools.reduce(jnp.maximum, rest)
        second, taken = [], None
        for f, x in zip(first, xs):
            c = jnp.logical_and(x == m2, jnp.logical_not(f))
            if taken is not None:
                c = jnp.logical_and(c, jnp.logical_not(taken))
            taken = c if taken is None else jnp.logical_or(taken, c)
            second.append(c)
        flags.append([jnp.logical_or(f, c) for f, c in zip(first, second)])
        scores.append(m1 + m2)
    best = functools.reduce(jnp.maximum, scores)
    taken, rows = None, []
    for g in range(N_EXPERT_GROUPS):
        bg = (scores[g] == best) if taken is None else jnp.logical_and(scores[g] == best, jnp.logical_not(taken))
        taken = bg if taken is None else jnp.logical_or(taken, bg)
        for i in range(EXPERTS_PER_GROUP):
            e = g * EXPERTS_PER_GROUP + i
            rows.append(jnp.where(jnp.logical_and(bg, flags[g][i]), s[e:e + 1], 0.0))
    denom = functools.reduce(jnp.add, rows)
    inv = 1.0 / denom
    return [r * inv for r in rows]


ROW_TILES = D_MODEL // LANES


def _to_token_rows(ref, x, lead=()):
    n = x.shape[0]
    for s in range(ROW_TILES):
        ref[lead + (pl.ds(s, n, stride=ROW_TILES), slice(None))] = x[:, s * LANES:(s + 1) * LANES]


def _from_token_rows(ref, n, lead=()):
    return jnp.concatenate([ref[lead + (pl.ds(s, n, stride=ROW_TILES), slice(None))] for s in range(ROW_TILES)],
                           axis=-1)


def _norm_mod_router_kernel(x_ref, g_ref, sc_ref, sh_ref, rw_ref, rb_ref, o_ref, gates_ref):
    h = _normed(x_ref[0], g_ref[...], sc_ref[0], sh_ref[0])
    _to_token_rows(o_ref, h, lead=(0,))
    logits = _dot3(rw_ref[...], h, NT_DIMS)
    for e, row in enumerate(_top2_gates(logits, rb_ref[...])):
        gates_ref[0, e:e + 1, :] = row


def _norm_mod(x, gain, sc, sh, tm=512):
    b, t, d = x.shape
    tm = min(tm, t)
    per_b = sc.shape[0] > 1
    mod_map = (lambda bi, i: (bi, 0, 0)) if per_b else (lambda bi, i: (0, 0, 0))
    return pl.pallas_call(
        _norm_mod_kernel,
        grid=(b, t // tm),
        in_specs=[pl.BlockSpec((1, tm, d), lambda bi, i: (bi, i, 0)),
                  pl.BlockSpec((1, d), lambda bi, i: (0, 0)),
                  pl.BlockSpec((1, 1, d), mod_map),
                  pl.BlockSpec((1, 1, d), mod_map)],
        out_specs=pl.BlockSpec((1, tm, d), lambda bi, i: (bi, i, 0)),
        out_shape=jax.ShapeDtypeStruct((b, t, d), BF16),
        compiler_params=_params(("parallel", "parallel")),
        name="norm_mod",
    )(x, gain.reshape(1, d), sc, sh)


def _resid_kernel(x_ref, f_ref, gt_ref, xo_ref):
    xo_ref[0] = x_ref[0] + gt_ref[0] * _from_token_rows(f_ref, x_ref.shape[1])


def _resid_norm_mod_kernel(x_ref, f_ref, gt_ref, g_ref, sc_ref, sh_ref, xo_ref, o_ref):
    x = x_ref[0] + gt_ref[0] * _from_token_rows(f_ref, x_ref.shape[1])
    xo_ref[0] = x
    o_ref[0] = _normed(x, g_ref[...], sc_ref[0], sh_ref[0]).astype(BF16)


def _resid(x, f_rows, gt, norm=None, tm=512):
    b, t, d = x.shape
    tm = min(tm, t)
    nt = t // tm
    tok = lambda: pl.BlockSpec((1, tm, d), lambda bi, i: (bi, i, 0))
    per_b = lambda: pl.BlockSpec((1, 1, d), lambda bi, i: (bi, 0, 0))
    in_specs = [tok(), pl.BlockSpec((tm * ROW_TILES, LANES), lambda bi, i: (bi * nt + i, 0)), per_b()]
    args = [x, f_rows, gt]
    if norm is None:
        return pl.pallas_call(
            _resid_kernel, grid=(b, nt), in_specs=in_specs, out_specs=tok(),
            out_shape=jax.ShapeDtypeStruct((b, t, d), F32),
            compiler_params=_params(("parallel", "parallel")), name="resid",
        )(*args)
    gain, sc, sh = norm
    return pl.pallas_call(
        _resid_norm_mod_kernel, grid=(b, nt),
        in_specs=in_specs + [pl.BlockSpec((1, d), lambda bi, i: (0, 0)), per_b(), per_b()],
        out_specs=[tok(), tok()],
        out_shape=[jax.ShapeDtypeStruct((b, t, d), F32), jax.ShapeDtypeStruct((b, t, d), BF16)],
        compiler_params=_params(("parallel", "parallel")), name="resid_norm_mod",
    )(*args, gain.reshape(1, d), sc, sh)


def _norm_mod_router(x, gain, sc, sh, router_w_t, router_b, tm=512):
    b, t, d = x.shape
    tm = min(tm, t)
    e = router_w_t.shape[0]
    return pl.pallas_call(
        _norm_mod_router_kernel,
        grid=(b, t // tm),
        in_specs=[pl.BlockSpec((1, tm, d), lambda bi, i: (bi, i, 0)),
                  pl.BlockSpec((1, d), lambda bi, i: (0, 0)),
                  pl.BlockSpec((1, 1, d), lambda bi, i: (bi, 0, 0)),
                  pl.BlockSpec((1, 1, d), lambda bi, i: (bi, 0, 0)),
                  pl.BlockSpec((e, d), lambda bi, i: (0, 0)),
                  pl.BlockSpec((e, 1), lambda bi, i: (0, 0))],
        out_specs=[pl.BlockSpec((1, tm * ROW_TILES, LANES), lambda bi, i: (bi, i, 0)),
                   pl.BlockSpec((1, e, tm), lambda bi, i: (bi, 0, i))],
        out_shape=[jax.ShapeDtypeStruct((b, t * ROW_TILES, LANES), F32),
                   jax.ShapeDtypeStruct((b, e, t), F32)],
        compiler_params=_params(("parallel", "parallel")),
        name="norm_mod_router",
    )(x, gain.reshape(1, d), sc, sh, router_w_t, router_b.reshape(e, 1))


def _mm_kernel(*refs, n_terms, a_batched, w_batched, has_res):
    a_refs = refs[:n_terms]
    w_refs = refs[n_terms:2 * n_terms]
    o_ref = refs[-1]
    acc = None
    for a_ref, w_ref, ab, wb in zip(a_refs, w_refs, a_batched, w_batched):
        a = a_ref[0] if ab else a_ref[...]
        w = w_ref[0] if wb else w_ref[...]
        p = jnp.dot(a, w, preferred_element_type=F32)
        acc = p if acc is None else acc + p
    if has_res:
        res_ref, gate_ref = refs[2 * n_terms], refs[2 * n_terms + 1]
        acc = res_ref[0] + gate_ref[0] * acc
    o_ref[0] = acc.astype(o_ref.dtype)


def _matmul(a_list, w_list, out_dtype, *, batch, rows, tm, tn, res=None, gate=None, name="matmul"):
    n = w_list[0].shape[-1]
    tm = min(tm, rows)
    tn = min(tn, n)
    a_batched = tuple(a.ndim == 3 for a in a_list)
    w_batched = tuple(w.ndim == 3 for w in w_list)
    in_specs = []
    for a, ab in zip(a_list, a_batched):
        k = a.shape[-1]
        in_specs.append(pl.BlockSpec((1, tm, k), lambda b, i, j: (b, i, 0)) if ab
                        else pl.BlockSpec((tm, k), lambda b, i, j: (i, 0)))
    for w, wb in zip(w_list, w_batched):
        k = w.shape[-2]
        in_specs.append(pl.BlockSpec((1, k, tn), lambda b, i, j: (b, 0, j)) if wb
                        else pl.BlockSpec((k, tn), lambda b, i, j: (0, j)))
    args = list(a_list) + list(w_list)
    if res is not None:
        in_specs.append(pl.BlockSpec((1, tm, tn), lambda b, i, j: (b, i, j)))
        in_specs.append(pl.BlockSpec((1, 1, tn), lambda b, i, j: (b, 0, j)))
        args += [res, gate]
    kern = functools.partial(_mm_kernel, n_terms=len(a_list), a_batched=a_batched, w_batched=w_batched,
                             has_res=res is not None)
    return pl.pallas_call(
        kern,
        grid=(batch, rows // tm, n // tn),
        in_specs=in_specs,
        out_specs=pl.BlockSpec((1, tm, tn), lambda b, i, j: (b, i, j)),
        out_shape=jax.ShapeDtypeStruct((batch, rows, n), out_dtype),
        compiler_params=_params(("parallel", "parallel", "parallel")),
        name=name,
    )(*args)


def _pair_ones():
    r = lax.broadcasted_iota(jnp.int32, (LANES, LANES), 0) // RWKV_HEAD
    c = lax.broadcasted_iota(jnp.int32, (LANES, LANES), 1) // RWKV_HEAD
    return jnp.where(r == c, 1.0, 0.0).astype(BF16)


def _head_sum(x, ones_bf16):
    return _dot_exact_rhs(x, ones_bf16)


def _token_shift(ref, s, n, t, mu_p, mu_n):
    x = ref[0, s:s + n, :]
    w = x.shape[-1]
    ridx = lax.broadcasted_iota(jnp.int32, (n, w), 0)
    prev_row = ref[0, s - 1:s, :] if s > 0 else jnp.zeros((1, w), F32)
    next_row = ref[0, s + n:s + n + 1, :] if s + n < t else jnp.zeros((1, w), F32)
    prev = jnp.where(ridx == 0, prev_row, pltpu.roll(x, 1, 0))
    nxt = jnp.where(ridx == n - 1, next_row, pltpu.roll(x, n - 1, 0))
    return x + mu_p * (prev - x) + mu_n * (nxt - x)


def _rwkv_prep_kernel(pr_ref, pk_ref, pv_ref, pl_ref, par_ref, lpar_ref, w2_ref, a2_ref, g2_ref,
                      r_ref, v_ref, kk_ref, g_ref, ks_ref, lw_ref, kd_ref, alr_ref, *, t, rows):
    ones = _pair_ones()
    par = par_ref[...]
    lpar = lpar_ref[...]
    k_k, k_a = par[10:11], par[11:12]
    for s in range(0, t, rows):
        sl = slice(s, s + rows)
        r = _token_shift(pr_ref, s, rows, t, par[0:1], par[1:2])
        k = _token_shift(pk_ref, s, rows, t, par[2:3], par[3:4])
        v = _token_shift(pv_ref, s, rows, t, par[4:5], par[5:6])
        lo = _token_shift(pl_ref, s, rows, t, lpar[0:1], lpar[1:2])
        xwa, xg = lo[:, :LANES], lo[:, LANES:]
        r_ref[0, sl, :] = r
        v_ref[0, sl, :] = v
        g_ref[0, sl, :] = _bdot(_sigmoid(xg), g2_ref[...])
        kk = k * k_k
        kk = kk * lax.rsqrt(_head_sum(kk * kk, ones) + KK_EPS)
        kk_ref[0, sl, :] = kk
        tw = jnp.tanh(xwa)
        xa = xwa
        ksum = None
        for d in range(2):
            z = -(par[6 + d:7 + d] + _bdot(tw, w2_ref[d]))
            softplus = jnp.maximum(z, 0.0) + jnp.log(1.0 + jnp.exp(-jnp.abs(z)))
            w_log = -softplus - 0.5
            lw_ref[d, 0, sl, :] = -jnp.exp(w_log)
            a = _sigmoid(par[8 + d:9 + d] + _bdot(xa, a2_ref[d]))
            alr_ref[d, 0, sl, :] = a
            k_d = k * (1.0 + (a - 1.0) * k_a)
            kd_ref[d, 0, sl, :] = k_d
            ksum = k_d if ksum is None else ksum + k_d
        ks_ref[0, sl, :] = ksum


def _rwkv_prep(pr, pk, pv, plo, par, lpar, w2, a2, g2):
    b, t, _ = pr.shape
    npair = D_A // LANES
    rows = min(256, t)
    lora = plo.shape[-1]
    slab = lambda: pl.BlockSpec((1, t, LANES), lambda bi, p: (bi, 0, p))
    dslab = lambda: pl.BlockSpec((2, 1, t, LANES), lambda bi, p: (0, bi, 0, p))
    one = jax.ShapeDtypeStruct((b, t, D_A), F32)
    two = jax.ShapeDtypeStruct((2, b, t, D_A), F32)
    return pl.pallas_call(
        functools.partial(_rwkv_prep_kernel, t=t, rows=rows),
        grid=(b, npair),
        in_specs=[slab(), slab(), slab(),
                  pl.BlockSpec((1, t, lora), lambda bi, p: (bi, 0, 0)),
                  pl.BlockSpec((16, LANES), lambda bi, p: (0, p)),
                  pl.BlockSpec((8, lora), lambda bi, p: (0, 0)),
                  pl.BlockSpec((2, LANES, LANES), lambda bi, p: (0, 0, p)),
                  pl.BlockSpec((2, LANES, LANES), lambda bi, p: (0, 0, p)),
                  pl.BlockSpec((G_LORA, LANES), lambda bi, p: (0, p))],
        out_specs=[slab(), slab(), slab(), slab(), slab(), dslab(), dslab(), dslab()],
        out_shape=[one, one, one, one, one, two, two, two],
        compiler_params=_params(("parallel", "parallel")),
        name="rwkv_prep",
    )(pr, pk, pv, plo, par, lpar, w2, a2, g2)


_gram_dot = _bdot
_inv_dot = _bdot
_wu_dot = _bdot
_state_dot = _bdot


def _wkv_local(chunks, sgn, consts):
    n = WKV_CHUNK
    n2 = 2 * n
    tri, keep, dt, eye, lvl = consts
    order = sgn * dt
    strict = order > 0
    incl = order >= 0
    tri_b = jnp.where(sgn * tri >= 0, 1.0, 0.0).astype(BF16)
    each = lambda f, *ls: [f(*xs) for xs in zip(*ls)]

    def stack(x):
        return jnp.where(keep, jnp.concatenate([x, x], axis=0), 0.0)

    lws = [c[1] for c in chunks]
    cums = each(lambda lw: _dot_exact_lhs(tri_b, lw), lws)
    ctots = each(lambda lw: jnp.sum(lw, axis=0, keepdims=True), lws)
    lhs, rhs, kh2s, bh2s, v2s, at2s, rt2s = [], [], [], [], [], [], []
    for (r, lw, k, v, kk, alr), cum, ctot in zip(chunks, cums, ctots):
        e_neg = jnp.exp(-cum)
        e_rem = jnp.exp(ctot - cum)
        b = kk * alr
        rt2 = stack(r * jnp.exp(cum))
        at2 = stack(-kk * jnp.exp(cum - lw))
        lhs.append(jnp.concatenate([at2, rt2], axis=0))
        rhs.append(jnp.concatenate([stack(b * e_neg), stack(k * e_neg)], axis=0))
        kh2s.append(stack(k * e_rem))
        bh2s.append(stack(b * e_rem))
        v2s.append(stack(v))
        at2s.append(at2)
        rt2s.append(rt2)
    gs = each(lambda x, y: _gram_dot(x, y, NT_DIMS), lhs, rhs)
    a_abs = [jnp.where(strict, g[:n2, :n2], 0.0) for g in gs]
    a_rbs = [jnp.where(incl, g[n2:, :n2], 0.0) for g in gs]
    a_kks = [jnp.concatenate([jnp.where(strict, g[:n2, n2:], 0.0), jnp.where(incl, g[n2:, n2:], 0.0)], axis=0)
             for g in gs]
    m0 = sgn * lvl[0] == 1
    tinvs = [eye + jnp.where(m0, a, 0.0) for a in a_abs]
    for m in lvl[1:]:
        msk = sgn * m == 1
        cts = each(lambda a, t: _inv_dot(jnp.where(msk, a, 0.0), t), a_abs, tinvs)
        tinvs = each(lambda t, ct: t + _inv_dot(t, ct), tinvs, cts)
    kvs = each(_bdot, a_kks, v2s)
    wus = each(lambda t, at2, kv: _wu_dot(t, jnp.concatenate([at2, kv[:n2]], axis=1)), tinvs, at2s, kvs)
    rys = each(lambda rt2, kv, a_rb, wu: jnp.concatenate([rt2, kv[n2:]], axis=1) + _bdot(a_rb, wu),
               rt2s, kvs, a_rbs, wus)
    bws = each(lambda bh2, wu: _bdot(bh2, wu, TN_DIMS), bh2s, wus)
    khv = each(lambda kh2, v2: _bdot(kh2, v2, TN_DIMS), kh2s, v2s)
    out = []
    for ry, bw, kv, ctot in zip(rys, bws, khv, ctots):
        decay_col = jnp.sum(jnp.where(eye > 0, jnp.exp(ctot), 0.0), axis=1, keepdims=True)
        out.append((ry[:, :LANES], ry[:, LANES:], bw[:, :LANES], bw[:, LANES:] + kv, decay_col))
    return out


def _wkv_consts():
    n = WKV_CHUNK
    n2 = 2 * n
    tri = lax.broadcasted_iota(jnp.int32, (n, n), 0) - lax.broadcasted_iota(jnp.int32, (n, n), 1)
    lane = lax.broadcasted_iota(jnp.int32, (n2, LANES), 1)
    rr = lax.broadcasted_iota(jnp.int32, (n2, LANES), 0)
    keep = (lane >= RWKV_HEAD) == (rr >= n)
    ri = lax.broadcasted_iota(jnp.int32, (n2, n2), 0)
    ci = lax.broadcasted_iota(jnp.int32, (n2, n2), 1)
    r2 = ri & (n - 1)
    c2 = ci & (n - 1)
    dt = r2 - c2
    eye = jnp.where(ri == ci, 1.0, 0.0).astype(F32)
    lvl = []
    k = 0
    while (1 << k) < n:
        same = (r2 >> (k + 1)) == (c2 >> (k + 1))
        diff = ((r2 >> k) & 1) - ((c2 >> k) & 1)
        lvl.append(jnp.where(same, diff, 0))
        k += 1
    return tri, keep, dt, eye, lvl


def _wkv_kernel(r_ref, v_ref, kk_ref, lw_ref, kd_ref, alr_ref, s0_ref, y_ref, sfin_ref, st_ref, *,
                n_sub, n_pairs):
    d = pl.program_id(0)
    c = pl.program_id(3)
    nc = pl.num_programs(3)

    @pl.when(c == 0)
    def _():
        st_ref[...] = s0_ref[0, 0]

    sgn = 1 - 2 * d
    rows = []
    for u in range(n_sub):
        ui = u + d * (n_sub - 1 - 2 * u)
        rows.append(pl.ds(pl.multiple_of(ui * WKV_CHUNK, WKV_CHUNK), WKV_CHUNK))
    lanes = [slice(p * LANES, (p + 1) * LANES) for p in range(n_pairs)]
    chunks = [(r_ref[0, rw, ln], lw_ref[0, 0, rw, ln], kd_ref[0, 0, rw, ln], v_ref[0, rw, ln], kk_ref[0, rw, ln],
               alr_ref[0, 0, rw, ln]) for ln in lanes for rw in rows]
    local = _wkv_local(chunks, sgn, _wkv_consts())
    for p, ln in enumerate(lanes):
        st = st_ref[p]
        for rw, (rhat, yhat, mmat, nmat, decay_col) in zip(rows, local[p * n_sub:(p + 1) * n_sub]):
            y2 = _state_dot(rhat, st) + yhat
            y_ref[0, 0, rw, ln] = y2[:WKV_CHUNK] + y2[WKV_CHUNK:]
            st = decay_col * st + _state_dot(mmat, st) + nmat
        st_ref[p] = st

        @pl.when(c == nc - 1)
        def _():
            sfin_ref[0, 0, p] = st


def _wkv(r, v, kk, lw, kd, alr, s0):
    b, t, _ = r.shape
    npair = D_A // LANES
    n_sub = min(WKV_SUB, t // WKV_CHUNK)
    n_pairs = WKV_SUB // n_sub
    blk = n_sub * WKV_CHUNK
    nc = t // blk
    wl = n_pairs * LANES

    def tmap(d, bi, p, c):
        return (bi, c + d * (nc - 1 - 2 * c), p)

    def dmap(d, bi, p, c):
        return (d, bi, c + d * (nc - 1 - 2 * c), p)

    smap = lambda d, bi, p, c: (d, bi, p, 0, 0)
    shared = lambda: pl.BlockSpec((1, blk, wl), tmap)
    direc = lambda: pl.BlockSpec((1, 1, blk, wl), dmap)
    state = lambda: pl.BlockSpec((1, 1, n_pairs, LANES, LANES), smap)
    return pl.pallas_call(
        functools.partial(_wkv_kernel, n_sub=n_sub, n_pairs=n_pairs),
        grid=(2, b, npair // n_pairs, nc),
        in_specs=[shared(), shared(), shared(), direc(), direc(), direc(), state()],
        out_specs=[direc(), state()],
        out_shape=[jax.ShapeDtypeStruct((2, b, t, D_A), F32),
                   jax.ShapeDtypeStruct((2, b, npair, LANES, LANES), F32)],
        scratch_shapes=[pltpu.VMEM((n_pairs, LANES, LANES), F32)],
        compiler_params=_params(("parallel", "parallel", "parallel", "arbitrary")),
        name="wkv",
    )(r, v, kk, lw, kd, alr, s0)


def _rwkv_finish_kernel(y_ref, r_ref, v_ref, ks_ref, g_ref, par_ref, o_ref):
    ones = _pair_ones()
    par = par_ref[...]
    inv_n = 1.0 / RWKV_HEAD
    for p in range(D_A // LANES):
        sl = slice(p * LANES, (p + 1) * LANES)
        y = y_ref[0, 0, :, sl] + y_ref[1, 0, :, sl]
        mu = _head_sum(y, ones) * inv_n
        yc = y - mu
        var = _head_sum(yc * yc, ones) * inv_n
        yn = yc * lax.rsqrt(var + LNX_EPS) * par[0:1, sl] + par[1:2, sl]
        bonus = _head_sum(r_ref[0, :, sl] * ks_ref[0, :, sl] * par[2:3, sl], ones) * v_ref[0, :, sl]
        o_ref[0, :, sl] = ((yn + bonus) * g_ref[0, :, sl]).astype(BF16)


def _rwkv_finish(y, r, v, ks, g, par, tm=256):
    b, t, d = r.shape
    tile = lambda: pl.BlockSpec((1, tm, d), lambda bi, i: (bi, i, 0))
    return pl.pallas_call(
        _rwkv_finish_kernel,
        grid=(b, t // tm),
        in_specs=[pl.BlockSpec((2, 1, tm, d), lambda bi, i: (0, bi, i, 0)), tile(), tile(), tile(), tile(),
                  pl.BlockSpec((8, d), lambda bi, i: (0, 0))],
        out_specs=tile(),
        out_shape=jax.ShapeDtypeStruct((b, t, d), BF16),
        compiler_params=_params(("parallel", "parallel")),
        name="rwkv_finish",
    )(y, r, v, ks, g, par)


def _head_rms(x, gain):
    return x * lax.rsqrt(jnp.mean(x * x, axis=-1, keepdims=True) + NORM_EPS) * gain


def _rope(x, cos_e, sin_s):
    lane = lax.broadcasted_iota(jnp.int32, x.shape, 1)
    swapped = jnp.where((lane & 1) == 0, pltpu.roll(x, LANES - 1, 1), pltpu.roll(x, 1, 1))
    return x * cos_e + swapped * sin_s


def _attn_prep_kernel(*refs, with_q):
    if with_q:
        pq_ref, pkv_ref, gains_ref, cos_ref, sin_ref, q_ref, k_ref, v_ref = refs
    else:
        pkv_ref, gains_ref, k_ref, v_ref = refs
    gains = gains_ref[...]
    kv_w = H_KV * HEAD_DIM
    for h in range(H_KV):
        sl = slice(h * HEAD_DIM, (h + 1) * HEAD_DIM)
        kh = _head_rms(pkv_ref[0, :, sl], gains[1:2])
        if with_q:
            kh = _rope(kh, cos_ref[...], sin_ref[...])
        k_ref[0, :, sl] = kh.astype(BF16)
    v_ref[0] = pkv_ref[0, :, kv_w:].astype(BF16)
    if with_q:
        scale = HEAD_DIM ** -0.5 * float(np.log2(np.e))
        for h in range(H_Q):
            sl = slice(h * HEAD_DIM, (h + 1) * HEAD_DIM)
            qh = _rope(_head_rms(pq_ref[0, :, sl], gains[0:1]), cos_ref[...], sin_ref[...])
            q_ref[0, :, sl] = (qh * scale).astype(BF16)


def _attn_prep(pq, pkv, gains, cos_e, sin_s, tm=256):
    b, t, _ = pkv.shape
    with_q = pq is not None
    kv_w = H_KV * HEAD_DIM
    tm = min(tm, t)
    kv_spec = lambda: pl.BlockSpec((1, tm, kv_w), lambda bi, i: (bi, i, 0))
    in_specs = [pl.BlockSpec((1, tm, 2 * kv_w), lambda bi, i: (bi, i, 0)),
                pl.BlockSpec((8, HEAD_DIM), lambda bi, i: (0, 0))]
    args = [pkv, gains]
    out_specs = [kv_spec(), kv_spec()]
    out_shape = [jax.ShapeDtypeStruct((b, t, kv_w), BF16)] * 2
    if with_q:
        in_specs = [pl.BlockSpec((1, tm, D_B), lambda bi, i: (bi, i, 0))] + in_specs + [
            pl.BlockSpec((tm, HEAD_DIM), lambda bi, i: (i, 0)),
            pl.BlockSpec((tm, HEAD_DIM), lambda bi, i: (i, 0))]
        args = [pq] + args + [cos_e, sin_s]
        out_specs = [pl.BlockSpec((1, tm, D_B), lambda bi, i: (bi, i, 0))] + out_specs
        out_shape = [jax.ShapeDtypeStruct((b, t, D_B), BF16)] + out_shape
    return pl.pallas_call(
        functools.partial(_attn_prep_kernel, with_q=with_q),
        grid=(b, t // tm),
        in_specs=in_specs,
        out_specs=out_specs,
        out_shape=out_shape,
        compiler_params=_params(("parallel", "parallel")),
        name="attn_prep",
    )(*args)


def _attn_kernel(q_ref, k_ref, v_ref, o_ref):
    k = k_ref[0]
    v = v_ref[0]
    for g in range(Q_PER_KV):
        sl = slice(g * HEAD_DIM, (g + 1) * HEAD_DIM)
        s = lax.dot_general(q_ref[0, :, sl], k, NT_DIMS, preferred_element_type=F32)
        p = jnp.exp2(s - jnp.max(s, axis=-1, keepdims=True))
        o = jnp.dot(p.astype(BF16), v, preferred_element_type=F32)
        o_ref[0, :, sl] = (o[:, :HEAD_DIM] / o[:, HEAD_DIM:HEAD_DIM + 1]).astype(BF16)


def _attention(q, k_all, v_all, tq=512):
    b, t, _ = q.shape
    s_len = k_all.shape[1]
    gw = Q_PER_KV * HEAD_DIM
    return pl.pallas_call(
        _attn_kernel,
        grid=(b, H_KV, t // tq),
        in_specs=[pl.BlockSpec((1, tq, gw), lambda bi, h, i: (bi, i, h)),
                  pl.BlockSpec((1, s_len, HEAD_DIM), lambda bi, h, i: (bi, 0, h)),
                  pl.BlockSpec((1, s_len, 2 * HEAD_DIM), lambda bi, h, i: (bi, 0, h))],
        out_specs=pl.BlockSpec((1, tq, gw), lambda bi, h, i: (bi, i, h)),
        out_shape=jax.ShapeDtypeStruct((b, t, H_Q * HEAD_DIM), BF16),
        compiler_params=_params(("parallel", "parallel", "parallel")),
        name="attention",
    )(q, k_all, v_all)


def _dft_channels_kernel(h_ref, w_ref, c_ref, s_ref):
    p = jnp.dot(h_ref[0], w_ref[...], preferred_element_type=F32)
    c_ref[0] = p[:, :FOURIER_GROUP_DIM].astype(BF16)
    s_ref[0] = p[:, FOURIER_GROUP_DIM:].astype(BF16)


def _dft_channels(h, w_cs, tm=512):
    b, t, d = h.shape
    tm = min(tm, t)
    gd = FOURIER_GROUP_DIM
    blk = lambda: pl.BlockSpec((1, tm, gd), lambda bi, i, g: (bi, i, g))
    return pl.pallas_call(
        _dft_channels_kernel,
        grid=(b, t // tm, d // gd),
        in_specs=[blk(), pl.BlockSpec((gd, 2 * gd), lambda bi, i, g: (0, 0))],
        out_specs=[blk(), blk()],
        out_shape=[jax.ShapeDtypeStruct((b, t, d), BF16)] * 2,
        compiler_params=_params(("parallel", "parallel", "parallel")),
        name="dft_channels",
    )(h, w_cs)


def _dft_mats(n):
    i = lax.broadcasted_iota(jnp.int32, (n, n), 0)
    j = lax.broadcasted_iota(jnp.int32, (n, n), 1)
    ang = ((i * j) % n).astype(F32) * (2.0 * np.pi / n)
    scale = n ** -0.5
    return jnp.cos(ang) * scale, jnp.sin(ang) * scale


MOE_TILE = 512


def _moe_kernel(src_ref, grp_ref, live_ref, h_hbm, gs_ref, wg_ref, wu_ref, wd_ref, o_hbm,
                rows_ref, h16_ref, acc_ref, gsem, ssem, *, tm, n_tok):
    i = pl.program_id(0)
    j = pl.program_id(1)
    n_live = live_ref[0]
    live = i < n_live
    slot = i % 2

    def row(ref, t):
        return ref.at[pl.ds(pl.multiple_of(t * ROW_TILES, ROW_TILES), ROW_TILES)]

    def start_gather(tile, sl):
        def issue(k, c):
            tok = jnp.maximum(src_ref[tile * tm + k], 0)
            pltpu.make_async_copy(row(h_hbm, tok), row(rows_ref.at[sl], k), gsem.at[sl]).start()
            return c
        lax.fori_loop(0, tm, issue, 0, unroll=8)

    def wait_gather(sl):
        pltpu.make_async_copy(h_hbm.at[pl.ds(0, tm * ROW_TILES)], rows_ref.at[sl], gsem.at[sl]).wait()

    def start_scatter(tile, sl):
        def issue(k, c):
            tok = src_ref[tile * tm + k]
            tok = jnp.where(tok < 0, n_tok + k, tok)
            pltpu.make_async_copy(row(rows_ref.at[sl], k), row(o_hbm, tok), ssem.at[sl]).start()
            return c
        lax.fori_loop(0, tm, issue, 0, unroll=8)

    def wait_scatter(sl):
        pltpu.make_async_copy(rows_ref.at[sl], o_hbm.at[pl.ds(0, tm * ROW_TILES)], ssem.at[sl]).wait()

    @pl.when(jnp.logical_and(live, j == 0))
    def _():
        @pl.when(i == 0)
        def _():
            start_gather(0, 0)
            rows_ref[1] = jnp.zeros(rows_ref.shape[1:], F32)
            spare = pltpu.make_async_copy(rows_ref.at[1], o_hbm.at[pl.ds(n_tok * ROW_TILES, tm * ROW_TILES)],
                                          ssem.at[1])
            spare.start()
            spare.wait()
        wait_gather(slot)
        h16_ref[...] = _from_token_rows(rows_ref, tm, lead=(slot,)).astype(BF16)
        acc_ref[...] = jnp.zeros_like(acc_ref)

    @pl.when(jnp.logical_and(live, j == 1))
    def _():
        @pl.when(i > 0)
        def _():
            wait_scatter(1 - slot)

        @pl.when(i + 1 < n_live)
        def _():
            start_gather(i + 1, 1 - slot)

    @pl.when(live)
    def _():
        h = h16_ref[...]
        u = _silu(_bdot(h, wg_ref[0, 0])) * _bdot(h, wu_ref[0, 0])
        lane = lax.broadcasted_iota(jnp.int32, gs_ref.shape, 1)
        gcol = jnp.sum(jnp.where(lane == j, gs_ref[...], 0.0), axis=1, keepdims=True)
        acc_ref[...] += gcol * _bdot(u, wd_ref[0, 0])

    @pl.when(jnp.logical_and(live, j == pl.num_programs(1) - 1))
    def _():
        _to_token_rows(rows_ref, acc_ref[...], lead=(slot,))
        start_scatter(i, slot)

        @pl.when(i == n_live - 1)
        def _():
            wait_scatter(slot)


def _route_meta(gates, tm):
    n = gates.shape[0]
    ng, epg = N_EXPERT_GROUPS, EXPERTS_PER_GROUP
    gg = gates.reshape(n, ng, epg)
    grp = jnp.argmax(jnp.max(gg, axis=-1) > 0, axis=-1).astype(jnp.int32)
    onehot = (grp[:, None] == jnp.arange(ng)[None, :]).astype(jnp.int32)
    counts = jnp.sum(onehot, axis=0)
    padded = ((counts + tm - 1) // tm) * tm
    ends = jnp.cumsum(padded)
    starts = ends - padded
    rank = jnp.sum((jnp.cumsum(onehot, axis=0) - onehot) * onehot, axis=1)
    pos = jnp.sum(starts[None, :] * onehot, axis=1) + rank
    p_total = n + ng * tm
    src = jnp.full((p_total,), -1, jnp.int32).at[pos].set(jnp.arange(n, dtype=jnp.int32))
    gsel = jnp.sum(gg * onehot[:, :, None].astype(F32), axis=1)
    gs_sorted = jnp.zeros((p_total, epg), F32).at[pos].set(gsel)
    n_live = ends[-1] // tm
    tile_start = jnp.minimum(jnp.arange(p_total // tm, dtype=jnp.int32), n_live - 1) * tm
    tile_grp = jnp.minimum(jnp.sum(tile_start[:, None] >= ends[None, :], axis=1), ng - 1).astype(jnp.int32)
    return src, gs_sorted, tile_grp, n_live.astype(jnp.int32).reshape(1)


def _moe(h_rows, gates, wg, wu, wd, layer, tm=MOE_TILE):
    n = gates.shape[0]
    d = D_MODEL
    _, ne, _, de = wg.shape
    epg = EXPERTS_PER_GROUP
    src, gs_sorted, tile_grp, n_live = _route_meta(gates, tm)
    n_tiles = src.shape[0] // tm

    def wmap(i, j, src, grp, live):
        return (layer, grp[i] * epg + jnp.where(i < live[0], j, epg - 1), 0, 0)

    grid_spec = pltpu.PrefetchScalarGridSpec(
        num_scalar_prefetch=3,
        grid=(n_tiles, epg),
        in_specs=[pl.BlockSpec(memory_space=pl.ANY),
                  pl.BlockSpec((tm, epg), lambda i, j, src, grp, live: (i, 0)),
                  pl.BlockSpec((1, 1, d, de), wmap),
                  pl.BlockSpec((1, 1, d, de), wmap),
                  pl.BlockSpec((1, 1, de, d), wmap)],
        out_specs=pl.BlockSpec(memory_space=pl.ANY),
        scratch_shapes=[pltpu.VMEM((2, tm * ROW_TILES, LANES), F32), pltpu.VMEM((tm, d), BF16),
                        pltpu.VMEM((tm, d), F32), pltpu.SemaphoreType.DMA((2,)), pltpu.SemaphoreType.DMA((2,))],
    )
    return pl.pallas_call(
        functools.partial(_moe_kernel, tm=tm, n_tok=n),
        grid_spec=grid_spec,
        out_shape=jax.ShapeDtypeStruct(((n + tm) * ROW_TILES, LANES), F32),
        compiler_params=pltpu.CompilerParams(dimension_semantics=("arbitrary", "arbitrary"),
                                             vmem_limit_bytes=VMEM_LIMIT, disable_bounds_checks=True),
        name="moe",
    )(src, tile_grp, n_live, h_rows, gs_sorted, wg, wu, wd)


def _rope_tables(n):
    rows = n // GRID_W
    row = jnp.repeat(jnp.arange(rows), GRID_W).astype(F32)
    col = jnp.tile(jnp.arange(GRID_W), rows).astype(F32)
    inv = ROPE_THETA ** (-jnp.arange(0, ROPE_AXIS_DIM, 2, dtype=F32) / ROPE_AXIS_DIM)
    ang = jnp.concatenate([row[:, None] * inv, col[:, None] * inv], axis=-1)
    cos_e = jnp.repeat(jnp.cos(ang), 2, axis=-1)
    sin_s = jnp.stack([-jnp.sin(ang), jnp.sin(ang)], axis=-1).reshape(n, HEAD_DIM)
    return cos_e, sin_s


def _pad_rows(rows, n):
    a = jnp.stack(rows, axis=0)
    return jnp.pad(a, ((0, n - a.shape[0]), (0, 0)))


def _enter_layer(x, pending, gain, sc, sh):
    if pending is None:
        return x, _norm_mod(x, gain, sc, sh)
    f_rows, gt = pending
    return _resid(x, f_rows, gt, norm=(gain, sc, sh))


def _even_mixer(x, pending, ctx, mods, gain, w_in, mu_prev, mu_next, w0, w2, a0, a2, g2, k_k, k_a, r_k, ln_g, ln_b,
                q_gain, k_gain, w_out):
    bsz, seq, _ = x.shape
    sh1, sc1, gt1, csh1, csc1 = mods
    x, h_lat = _enter_layer(x, pending, gain, sc1, sh1)
    h_ctx = _norm_mod(ctx, gain, csc1, csh1)

    a_cols = 3 * D_A + W_LORA + A_LORA + G_LORA
    wb = w_in.astype(BF16)
    w_r, w_k, w_v = wb[:, :D_A], wb[:, D_A:2 * D_A], wb[:, 2 * D_A:3 * D_A]
    w_lo = wb[:, 3 * D_A:a_cols]
    w_q = wb[:, a_cols:a_cols + D_B]
    w_kv = wb[:, a_cols + D_B:]

    def proj(h, w, name):
        rows = h.shape[1]
        return _matmul([h], [w], F32, batch=bsz, rows=rows, tm=1024, tn=512, name=name)

    par = _pad_rows([mu_prev[:D_A], mu_next[:D_A], mu_prev[D_A:2 * D_A], mu_next[D_A:2 * D_A],
                     mu_prev[2 * D_A:3 * D_A], mu_next[2 * D_A:3 * D_A],
                     w0[0], w0[1], a0[0], a0[1], k_k, k_a], 16)
    lpar = _pad_rows([mu_prev[3 * D_A:], mu_next[3 * D_A:]], 8)
    fin_par = _pad_rows([ln_g, ln_b, r_k.reshape(-1)], 8)
    w2 = jnp.pad(w2, ((0, 0), (0, A_LORA), (0, 0)))
    a2 = jnp.pad(a2, ((0, 0), (W_LORA, 0), (0, 0)))

    def streams(h):
        return _rwkv_prep(proj(h, w_r, "proj_r"), proj(h, w_k, "proj_k"), proj(h, w_v, "proj_v"),
                          proj(h, w_lo, "proj_lora"), par, lpar, w2, a2, g2)

    rc, vc, kkc, _, _, lwc, kdc, alrc = streams(h_ctx)
    rl, vl, kkl, gl, ksl, lwl, kdl, alrl = streams(h_lat)
    s0 = jnp.zeros((2, bsz, D_A // LANES, LANES, LANES), F32)
    _, s_ctx = _wkv(rc, vc, kkc, lwc, kdc, alrc, s0)
    y, _ = _wkv(rl, vl, kkl, lwl, kdl, alrl, s_ctx)
    ya = _rwkv_finish(y, rl, vl, ksl, gl, fin_par)

    cos_e, sin_s = _rope_tables(seq)
    gains = _pad_rows([q_gain, k_gain], 8)
    k_c, v_c = _attn_prep(None, proj(h_ctx, w_kv, "proj_kv"), gains, None, None)
    q_l, k_l, v_l = _attn_prep(proj(h_lat, w_q, "proj_q"), proj(h_lat, w_kv, "proj_kv"), gains, cos_e, sin_s)
    v_all = jnp.concatenate([v_c, v_l], axis=1).reshape(bsz, -1, H_KV, HEAD_DIM)
    v_ext = jnp.concatenate([v_all, jnp.ones_like(v_all)], axis=-1).reshape(bsz, -1, 2 * H_KV * HEAD_DIM)
    yb = _attention(q_l, jnp.concatenate([k_c, k_l], axis=1), v_ext)

    wo = w_out.astype(BF16)
    return _matmul([ya, yb], [wo[:D_A], wo[D_A:]], F32, batch=bsz, rows=seq, tm=1024, tn=512,
                   res=x, gate=gt1, name="mixer_out")


def _fourier_mixer(x, pending, mods, gain, w_out):
    bsz, seq, d = x.shape
    sh1, sc1, gt1 = mods
    x, h = _enter_layer(x, pending, gain, sc1, sh1)
    cc, sc = _dft_mats(FOURIER_GROUP_DIM)
    ct, st = _dft_mats(seq)
    pc, ps = _dft_channels(h, jnp.concatenate([cc, sc], axis=1).astype(BF16))
    f = _matmul([ct.astype(BF16), (-st).astype(BF16)], [pc, ps], BF16, batch=bsz, rows=seq, tm=1024, tn=1024,
                name="dft_tokens")
    return _matmul([f], [w_out.astype(BF16)], F32, batch=bsz, rows=seq, tm=1024, tn=512, res=x, gate=gt1,
                   name="mixer_out")


def _moe_layer(x, mods, gain, router_w_t, router_b, w_gate, w_up, w_down, layer):
    bsz, seq, d = x.shape
    sh2, sc2, _ = mods
    h_rows, gates_t = _norm_mod_router(x, gain, sc2, sh2, router_w_t, router_b)
    gates = jnp.swapaxes(gates_t, 1, 2).reshape(bsz * seq, -1)
    return _moe(h_rows.reshape(bsz * seq * ROW_TILES, LANES), gates, w_gate, w_up, w_down, layer)


def kernel(x, c, ctx, c_ctx, mod_w, mod_b, norm_mix, norm_ffn, ab_w_in, ab_mu_prev, ab_mu_next,
           rwkv_w0, rwkv_w2, rwkv_a0, rwkv_a2, rwkv_g2, rwkv_k_k, rwkv_k_a, rwkv_r_k, rwkv_ln_g, rwkv_ln_b,
           attn_q_norm, attn_k_norm, ab_w_out, fourier_w_out, router_w, router_b,
           exp_w_gate, exp_w_up, exp_w_down):
    bsz, seq, d = x.shape
    depth = mod_w.shape[0]
    cvec = jnp.concatenate([c, c_ctx[None], jnp.zeros((8 - bsz - 1, d), F32)], axis=0)
    mod = _modulation(cvec, mod_w, mod_b)
    router_w_t = router_w.T
    pending = None
    for l in range(depth):
        m_lat = mod[l, :bsz, None, :]
        sh1, sc1, gt1, sh2, sc2, gt2 = jnp.split(m_lat, 6, axis=-1)
        if l % 2 == 0:
            e = l // 2
            m_ctx = mod[l, bsz:bsz + 1, None, :]
            csh1, csc1 = m_ctx[..., :d], m_ctx[..., d:2 * d]
            x = _even_mixer(x, pending, ctx, (sh1, sc1, gt1, csh1, csc1), norm_mix[l], ab_w_in[e], ab_mu_prev[e],
                            ab_mu_next[e], rwkv_w0[e], rwkv_w2[e], rwkv_a0[e], rwkv_a2[e], rwkv_g2[e],
                            rwkv_k_k[e], rwkv_k_a[e], rwkv_r_k[e], rwkv_ln_g[e], rwkv_ln_b[e],
                            attn_q_norm[e], attn_k_norm[e], ab_w_out[e])
        else:
            x = _fourier_mixer(x, pending, (sh1, sc1, gt1), norm_mix[l], fourier_w_out[l // 2])
        f_rows = _moe_layer(x, (sh2, sc2, gt2), norm_ffn[l], router_w_t, router_b,
                            exp_w_gate, exp_w_up, exp_w_down, l)
        pending = (f_rows, gt2)
    return _resid(x, pending[0], pending[1])
```
